```python
import math, functools
import jax, jax.numpy as jnp
from jax import lax
import numpy as np

D_MODEL = 1024
BATCH = 8
SEQ = 2048
DEPTH = 1
DEC_BATCH = 128
DEC_SEQ = 8
PAST_LEN = 8192
PAGE_SIZE = 128

N_HEADS = 8
HEAD_DIM = 64
V_DIM = 2 * HEAD_DIM
ATT_W = N_HEADS * 2 * HEAD_DIM
ROT_DIM = HEAD_DIM // 4
ROPE_THETA = 500000.0
SGU_GROUPS = 8
CHUNK = 128
D_SGU = 1024
SGU_GROUP_W = D_SGU // SGU_GROUPS
D_FF = 2816
CONV_W = 3
Q_BLOCK = 128
EPS = 1e-6
SPLITS = (ATT_W, 2 * ATT_W, 3 * ATT_W, 3 * ATT_W + D_SGU, 3 * ATT_W + 2 * D_SGU,
          3 * ATT_W + 2 * D_SGU + D_MODEL)
D_IN = 3 * ATT_W + 2 * D_SGU + 2 * D_MODEL

kernel_name = 'gated_diffattn_chunk_sgu_convffn_step'


def rmsnorm(x, g):
    xf = x.astype(jnp.float32)
    y = xf * lax.rsqrt(jnp.mean(xf * xf, axis=-1, keepdims=True) + EPS)
    return (y * g.astype(jnp.float32)).astype(x.dtype)


def layernorm(x, g, b):
    xf = x.astype(jnp.float32)
    mu = jnp.mean(xf, axis=-1, keepdims=True)
    xc = xf - mu
    y = xc * lax.rsqrt(jnp.mean(xc * xc, axis=-1, keepdims=True) + EPS)
    return (y * g.astype(jnp.float32) + b.astype(jnp.float32)).astype(x.dtype)


def lambda_init(layer):
    return 0.8 - 0.6 * math.exp(-0.3 * layer)


def diff_lambda(lq1, lk1, lq2, lk2, lam0):
    f = jnp.float32
    return (jnp.exp(jnp.sum(lq1.astype(f) * lk1.astype(f)))
            - jnp.exp(jnp.sum(lq2.astype(f) * lk2.astype(f))) + lam0)


def rope(x, pos):
    half = ROT_DIM // 2
    inv = ROPE_THETA ** (-jnp.arange(0, ROT_DIM, 2, dtype=jnp.float32) / ROT_DIM)
    ang = pos.astype(jnp.float32)[:, None] * inv[None, :]
    cos = jnp.cos(ang)[None, :, None, None, :]
    sin = jnp.sin(ang)[None, :, None, None, :]
    xr = x[..., :ROT_DIM].astype(jnp.float32)
    x1, x2 = xr[..., :half], xr[..., half:]
    rot = jnp.concatenate([x1 * cos - x2 * sin, x2 * cos + x1 * sin], axis=-1)
    return jnp.concatenate([rot.astype(x.dtype), x[..., ROT_DIM:]], axis=-1)


def diff_attn_prompt(q, k, v, lam):
    B, S = q.shape[0], q.shape[1]
    nq = S // Q_BLOCK
    qb = q.reshape(B, nq, Q_BLOCK, N_HEADS, 2, HEAD_DIM).transpose(1, 0, 2, 3, 4, 5)
    kpos = jnp.arange(S)

    def one_block(args):
        qblk, i = args
        s = jnp.einsum('bthcd,bkhcd->bhctk', qblk, k, preferred_element_type=jnp.float32)
        qpos = i * Q_BLOCK + jnp.arange(Q_BLOCK)
        s = jnp.where(kpos[None, :] <= qpos[:, None], s, -jnp.inf)
        p = jax.nn.softmax(s, axis=-1)
        w = p[:, :, 0] - lam * p[:, :, 1]
        return jnp.einsum('bhtk,bkhe->bthe', w, v.astype(jnp.float32))

    o = lax.map(one_block, (qb, jnp.arange(nq)))
    return o.transpose(1, 0, 2, 3, 4).reshape(B, S, N_HEADS, V_DIM)


def diff_attn_sample(q, k, v, cache_k, cache_v, layer, page_table, lam):
    DB, T = q.shape[0], q.shape[1]
    m0 = jnp.full((DB, N_HEADS, 2, T), -1e30, jnp.float32)
    l0 = jnp.zeros((DB, N_HEADS, 2, T), jnp.float32)
    a0 = jnp.zeros((DB, N_HEADS, 2, T, V_DIM), jnp.float32)

    def accumulate(carry, s, vb):
        m, l, a = carry
        m_new = jnp.maximum(m, jnp.max(s, axis=-1))
        corr = jnp.exp(m - m_new)
        p = jnp.exp(s - m_new[..., None])
        a = a * corr[..., None] + jnp.einsum('bhctk,bkhe->bhcte', p, vb.astype(jnp.float32))
        return (m_new, l * corr + jnp.sum(p, axis=-1), a)

    def page_step(carry, phys):
        kb = cache_k[layer, phys]
        vb = cache_v[layer, phys]
        s = jnp.einsum('bthcd,bkhcd->bhctk', q, kb, preferred_element_type=jnp.float32)
        return accumulate(carry, s, vb), None

    carry, _ = lax.scan(page_step, (m0, l0, a0), page_table.T)
    s = jnp.einsum('bthcd,bkhcd->bhctk', q, k, preferred_element_type=jnp.float32)
    tpos = jnp.arange(T)
    s = jnp.where(tpos[None, :] <= tpos[:, None], s, -jnp.inf)
    m, l, a = accumulate(carry, s, v)
    o = a[:, :, 0] / l[:, :, 0, :, None] - lam * (a[:, :, 1] / l[:, :, 1, :, None])
    return o.transpose(0, 2, 1, 3)


def spatial_mix(vn, w_s, b_s, n_chunks, chunk_len):
    B = vn.shape[0]
    vr = vn.reshape(B, n_chunks, chunk_len, SGU_GROUPS, SGU_GROUP_W)
    wm = (w_s * jnp.tril(jnp.ones((CHUNK, CHUNK), w_s.dtype)))[:, :chunk_len, :chunk_len]
    z = jnp.einsum('gij,bnjgc->bnigc', wm, vr) + b_s[:, :chunk_len].T[None, None, :, :, None]
    return z.reshape(B, n_chunks * chunk_len, D_SGU)


def mixer(x, pos, attend, n_chunks, chunk_len, lam0, w_in, w_o, g_pre, g_post, g_sub,
          ln_g, ln_b, w_s, b_s):
    B, T = x.shape[0], x.shape[1]
    xn = rmsnorm(x, g_pre)
    pq, pk, pv, pu, pvs, pga, pgb = jnp.split(xn @ w_in, SPLITS, axis=-1)
    q = rope(pq.reshape(B, T, N_HEADS, 2, HEAD_DIM), pos) * (HEAD_DIM ** -0.5)
    k = rope(pk.reshape(B, T, N_HEADS, 2, HEAD_DIM), pos)
    v = pv.reshape(B, T, N_HEADS, V_DIM)
    o = attend(q, k, v)
    o = (rmsnorm(o, g_sub) * (1.0 - lam0)).astype(x.dtype).reshape(B, T, ATT_W)
    u = jax.nn.gelu(pu)
    vn = layernorm(jax.nn.gelu(pvs), ln_g, ln_b)
    s_out = u * spatial_mix(vn, w_s, b_s, n_chunks, chunk_len)
    merged = jax.nn.sigmoid(pga) * o + jax.nn.sigmoid(pgb) * s_out
    y = merged @ w_o
    return x + rmsnorm(y, g_post), k, v, vn


def ffn(h, prev, g_pre, g_post, w_up, conv_w, conv_b, w_down):
    T = h.shape[1]
    hn = rmsnorm(h, g_pre)
    a, b = jnp.split(hn @ w_up, 2, axis=-1)
    xx = jnp.concatenate([prev.astype(a.dtype), a], axis=1)
    c = conv_b
    for j in range(CONV_W):
        c = c + conv_w[j] * xx[:, j:j + T]
    out = (jax.nn.gelu(c) * b) @ w_down
    return h + rmsnorm(out, g_post), xx[:, xx.shape[1] - (CONV_W - 1):]


def setup_inputs(seed: int = 0) -> dict:
    key = jax.random.key(seed)
    ks = jax.random.split(key, 32)
    f = jnp.float32
    n_pages = PAST_LEN // PAGE_SIZE
    n_pool = (5 * DEC_BATCH * n_pages) // 4
    nrm = lambda k, shape, s: jax.random.normal(k, shape, f) * s
    gain = lambda k, shape: 1.0 + 0.02 * jax.random.normal(k, shape, f)
    perm = jax.random.permutation(ks[5], n_pool)[:DEC_BATCH * n_pages]
    return {
        'x_prompt': nrm(ks[0], (BATCH, SEQ, D_MODEL), 1.0),
        'x_sample': nrm(ks[1], (DEC_BATCH, DEC_SEQ, D_MODEL), 1.0),
        'cache_k': nrm(ks[2], (DEPTH, n_pool, PAGE_SIZE, N_HEADS, 2, HEAD_DIM), 1.0),
        'cache_v': nrm(ks[3], (DEPTH, n_pool, PAGE_SIZE, N_HEADS, V_DIM), 1.0),
        'state_conv': nrm(ks[4], (DEPTH, DEC_BATCH, CONV_W - 1, D_FF), 1.0),
        'page_table': perm.reshape(DEC_BATCH, n_pages).astype(jnp.int32),
        'w_in': nrm(ks[6], (DEPTH, D_MODEL, D_IN), D_MODEL ** -0.5),
        'w_o': nrm(ks[7], (DEPTH, D_MODEL, D_MODEL), D_MODEL ** -0.5),
        'g_mix_pre': gain(ks[8], (DEPTH, D_MODEL)),
        'g_mix_post': gain(ks[9], (DEPTH, D_MODEL)),
        'lambda_q1': nrm(ks[10], (DEPTH, HEAD_DIM), 0.1),
        'lambda_k1': nrm(ks[11], (DEPTH, HEAD_DIM), 0.1),
        'lambda_q2': nrm(ks[12], (DEPTH, HEAD_DIM), 0.1),
        'lambda_k2': nrm(ks[13], (DEPTH, HEAD_DIM), 0.1),
        'g_subln': gain(ks[14], (DEPTH, V_DIM)),
        'sgu_ln_g': gain(ks[15], (DEPTH, D_SGU)),
        'sgu_ln_b': nrm(ks[16], (DEPTH, D_SGU), 0.02),
        'sgu_w': nrm(ks[17], (DEPTH, SGU_GROUPS, CHUNK, CHUNK), CHUNK ** -0.5),
        'sgu_b': gain(ks[18], (DEPTH, SGU_GROUPS, CHUNK)),
        'g_ffn_pre': gain(ks[19], (DEPTH, D_MODEL)),
        'g_ffn_post': gain(ks[20], (DEPTH, D_MODEL)),
        'w_up': nrm(ks[21], (DEPTH, D_MODEL, 2 * D_FF), D_MODEL ** -0.5),
        'conv_w': nrm(ks[22], (DEPTH, CONV_W, D_FF), CONV_W ** -0.5),
        'conv_b': nrm(ks[23], (DEPTH, D_FF), 0.02),
        'w_down': nrm(ks[24], (DEPTH, D_FF, D_MODEL), D_FF ** -0.5),
    }


def reference(x_prompt, x_sample, cache_k, cache_v, state_conv, page_table, w_in, w_o,
              g_mix_pre, g_mix_post, lambda_q1, lambda_k1, lambda_q2, lambda_k2, g_subln,
              sgu_ln_g, sgu_ln_b, sgu_w, sgu_b, g_ffn_pre, g_ffn_post, w_up, conv_w, conv_b,
              w_down):
    B, S = x_prompt.shape[0], x_prompt.shape[1]
    T = x_sample.shape[1]
    pos_p = jnp.arange(S)
    pos_s = PAST_LEN + jnp.arange(T)
    yp, ys = x_prompt, x_sample
    kp_l, vp_l, cp_l, ks_l, vs_l, us_l, cs_l = [], [], [], [], [], [], []
    for l in range(DEPTH):
        lam0 = lambda_init(l)
        lam = diff_lambda(lambda_q1[l], lambda_k1[l], lambda_q2[l], lambda_k2[l], lam0)
        mix_w = (w_in[l], w_o[l], g_mix_pre[l], g_mix_post[l], g_subln[l], sgu_ln_g[l],
                 sgu_ln_b[l], sgu_w[l], sgu_b[l])
        ffn_w = (g_ffn_pre[l], g_ffn_post[l], w_up[l], conv_w[l], conv_b[l], w_down[l])
        attend_p = functools.partial(diff_attn_prompt, lam=lam)
        hp, kp, vp, _ = mixer(yp, pos_p, attend_p, S // CHUNK, CHUNK, lam0, *mix_w)
        yp, cp = ffn(hp, jnp.zeros((B, CONV_W - 1, D_FF), hp.dtype), *ffn_w)
        attend_s = functools.partial(diff_attn_sample, cache_k=cache_k, cache_v=cache_v,
                                     layer=l, page_table=page_table, lam=lam)
        hs, ksm, vsm, us = mixer(ys, pos_s, attend_s, 1, T, lam0, *mix_w)
        ys, cs = ffn(hs, state_conv[l], *ffn_w)
        kp_l.append(kp); vp_l.append(vp); cp_l.append(cp)
        ks_l.append(ksm); vs_l.append(vsm); us_l.append(us); cs_l.append(cs)
    return (yp, ys, jnp.stack(kp_l), jnp.stack(vp_l), jnp.stack(cp_l), jnp.stack(ks_l),
            jnp.stack(vs_l), jnp.stack(us_l), jnp.stack(cs_l))
```

```python
import functools
import math

import jax
import jax.numpy as jnp
from jax import lax
from jax.experimental import pallas as pl
from jax.experimental.pallas import tpu as pltpu

F32 = jnp.float32
BF16 = jnp.bfloat16

D_MODEL = 1024
N_HEADS = 8
HEAD_DIM = 64
V_DIM = 2 * HEAD_DIM
ATT_W = N_HEADS * 2 * HEAD_DIM
ROT_DIM = HEAD_DIM // 4
ROPE_THETA = 500000.0
SGU_GROUPS = 8
CHUNK = 128
D_SGU = 1024
D_FF = 2816
CONV_W = 3
EPS = 1e-6
PAGE_SIZE = 128
D_IN = 3 * ATT_W + 2 * D_SGU + 2 * D_MODEL
NEG = -1e30

LANES = 128
SUBLANES = 8
VMEM_LIMIT = 56 * 1024 * 1024

TM = 256
TQ = 256
PAGES_PER_STEP = 4
NEW_PAD = 16


def _rms(x, g):
    return x * lax.rsqrt(jnp.mean(x * x, axis=-1, keepdims=True) + EPS) * g


def _nt_dot(a, b):
    return lax.dot_general(a, b, (((1,), (1,)), ((), ())), preferred_element_type=F32)


def _full(shape):
    return pl.BlockSpec(shape, lambda i: (0,) * len(shape), pipeline_mode=pl.Buffered(1))


def _in_proj_kernel(x_ref, w_ref, gpre_ref, lng_ref, lnb_ref, cos_ref, sa_ref, sb_ref,
                    q_ref, kf_ref, kb_ref, vf_ref, vb_ref, u_ref, vn_ref, ga_ref, gb_ref, *, k_transposed):
    xn = _rms(x_ref[...], gpre_ref[...]).astype(BF16)

    def proj(seg):
        return jnp.dot(xn, w_ref[:, seg * ATT_W:(seg + 1) * ATT_W], preferred_element_type=F32)

    cos, sa, sb = cos_ref[...], sa_ref[...], sb_ref[...]

    def rope(p, h):
        s = p[:, h * V_DIM:(h + 1) * V_DIM]
        return s * cos + pltpu.roll(s, LANES - ROT_DIM // 2, 1) * sa + pltpu.roll(s, ROT_DIM // 2, 1) * sb

    p = proj(0)
    for h in range(N_HEADS):
        q_ref[:, h * V_DIM:(h + 1) * V_DIM] = (rope(p, h) * (HEAD_DIM ** -0.5)).astype(q_ref.dtype)
    p = proj(1)
    for h in range(N_HEADS):
        r = rope(p, h)
        if k_transposed:
            rt = r.T
            kf_ref[h * V_DIM:(h + 1) * V_DIM, :] = rt
            kb_ref[h * V_DIM:(h + 1) * V_DIM, :] = rt.astype(BF16)
        else:
            kf_ref[:, h * V_DIM:(h + 1) * V_DIM] = r
            kb_ref[:, h * V_DIM:(h + 1) * V_DIM] = r.astype(BF16)
    p = proj(2)
    vf_ref[...] = p
    vb_ref[...] = p.astype(BF16)
    u_ref[...] = jax.nn.gelu(proj(3)).astype(BF16)
    g = jax.nn.gelu(proj(4))
    mu = jnp.mean(g, axis=-1, keepdims=True)
    gc = g - mu
    vn = gc * lax.rsqrt(jnp.mean(gc * gc, axis=-1, keepdims=True) + EPS) * lng_ref[...] + lnb_ref[...]
    vn_ref[...] = vn.astype(vn_ref.dtype)
    ga_ref[...] = jax.nn.sigmoid(proj(5)).astype(BF16)
    gb_ref[...] = jax.nn.sigmoid(proj(6)).astype(BF16)


def _in_proj(x, w_in, g_pre, ln_g, ln_b, tabs, *, q_dtype, vn_dtype, seq_len=None):
    n = x.shape[0]
    n_tab = tabs[0].shape[0] // TM
    row = lambda i: (i, 0)
    tab = lambda i: (i % n_tab, 0)
    act = pl.BlockSpec((TM, D_MODEL), row)
    outs = [(q_dtype, ATT_W), (F32, ATT_W), (BF16, ATT_W), (F32, ATT_W), (BF16, ATT_W),
            (BF16, D_SGU), (vn_dtype, D_SGU), (BF16, D_MODEL), (BF16, D_MODEL)]
    out_specs = [pl.BlockSpec((TM, w), row) for _, w in outs]
    out_shape = [jax.ShapeDtypeStruct((n, w), d) for d, w in outs]
    if seq_len is not None:
        tpr = seq_len // TM
        for idx in (1, 2):
            out_specs[idx] = pl.BlockSpec((None, ATT_W, TM), lambda i: (i // tpr, 0, i % tpr))
            out_shape[idx] = jax.ShapeDtypeStruct((n // seq_len, ATT_W, seq_len), outs[idx][0])
    return pl.pallas_call(
        functools.partial(_in_proj_kernel, k_transposed=seq_len is not None),
        grid=(n // TM,),
        in_specs=[act, _full((D_MODEL, D_IN)), _full((1, D_MODEL)), _full((1, D_SGU)), _full((1, D_SGU)),
                  pl.BlockSpec((TM, LANES), tab), pl.BlockSpec((TM, LANES), tab), pl.BlockSpec((TM, LANES), tab)],
        out_specs=out_specs,
        out_shape=out_shape,
        compiler_params=pltpu.CompilerParams(dimension_semantics=("parallel",), vmem_limit_bytes=VMEM_LIMIT),
        name="in_proj",
    )(x, w_in, g_pre, ln_g, ln_b, *tabs)


def _rope_tables(pos):
    half = ROT_DIM // 2
    inv = ROPE_THETA ** (-jnp.arange(0, ROT_DIM, 2, dtype=F32) / ROT_DIM)
    ang = pos.astype(F32)[:, None] * inv[None, :]
    cos, sin = jnp.cos(ang), jnp.sin(ang)
    d = jnp.arange(LANES) % HEAD_DIM
    lo, hi = d < half, (d >= half) & (d < ROT_DIM)
    idx = d % half
    c = jnp.where((lo | hi)[None, :], cos[:, idx], 1.0)
    sa = jnp.where(lo[None, :], -sin[:, idx], 0.0)
    sb = jnp.where(hi[None, :], sin[:, idx], 0.0)
    return c, sa, sb


def _diff_lambda(lq1, lk1, lq2, lk2, lam0):
    s1 = jnp.sum(lq1[...] * lk1[...], axis=-1, keepdims=True)
    s2 = jnp.sum(lq2[...] * lk2[...], axis=-1, keepdims=True)
    return jnp.exp(s1) - jnp.exp(s2) + lam0


def _subln(o, g, lam0):
    return _rms(o, g) * (1.0 - lam0)


def _prompt_attn_kernel(q_ref, k_ref, v_ref, lq1, lk1, lq2, lk2, gsub_ref, o_ref,
                        m_ref, l_ref, acc_ref, *, lam0):
    qi = pl.program_id(2)
    q = q_ref[...]
    lane = lax.broadcasted_iota(jnp.int32, q.shape, 1)
    zero = jnp.zeros_like(q)
    qs = jnp.concatenate([jnp.where(lane < HEAD_DIM, q, zero), jnp.where(lane >= HEAD_DIM, q, zero)], axis=0)

    m_ref[...] = jnp.full(m_ref.shape, NEG, F32)
    l_ref[...] = jnp.zeros(l_ref.shape, F32)
    acc_ref[...] = jnp.zeros(acc_ref.shape, F32)

    def step(kv, masked):
        start = pl.multiple_of(kv * TQ, TQ)
        s = jnp.dot(qs, k_ref[:, pl.ds(start, TQ)], preferred_element_type=F32)
        if masked:
            row = lax.broadcasted_iota(jnp.int32, s.shape, 0) % TQ
            col = lax.broadcasted_iota(jnp.int32, s.shape, 1)
            s = jnp.where(col <= row, s, NEG)
        m_old = m_ref[...]
        m_new = jnp.maximum(m_old, jnp.max(s, axis=1, keepdims=True))
        corr = jnp.exp(m_old - m_new)
        p = jnp.exp(s - m_new)
        l_ref[...] = l_ref[...] * corr + jnp.sum(p, axis=1, keepdims=True)
        acc_ref[...] = acc_ref[...] * corr + jnp.dot(p.astype(BF16), v_ref[pl.ds(start, TQ), :],
                                                     preferred_element_type=F32)
        m_ref[...] = m_new

    def body(kv, carry):
        step(kv, False)
        return carry

    lax.fori_loop(0, qi, body, 0)
    step(qi, True)

    lam = _diff_lambda(lq1, lk1, lq2, lk2, lam0)
    nrm = acc_ref[...] / l_ref[...]
    o = nrm[:TQ] - lam * nrm[TQ:]
    o_ref[...] = _subln(o, gsub_ref[...], lam0).astype(o_ref.dtype)


def _prompt_attn(q, kt, v, lams, g_sub, lam0):
    b, s, _ = q.shape
    qblk = pl.BlockSpec((None, TQ, V_DIM), lambda bi, h, qi: (bi, qi, h))
    kblk = pl.BlockSpec((None, V_DIM, s), lambda bi, h, qi: (bi, h, 0))
    vblk = pl.BlockSpec((None, s, V_DIM), lambda bi, h, qi: (bi, 0, h))
    small = lambda shape: pl.BlockSpec(shape, lambda bi, h, qi: (0, 0))
    return pl.pallas_call(
        functools.partial(_prompt_attn_kernel, lam0=lam0),
        grid=(b, N_HEADS, s // TQ),
        in_specs=[qblk, kblk, vblk] + [small((1, HEAD_DIM))] * 4 + [small((1, V_DIM))],
        out_specs=qblk,
        out_shape=jax.ShapeDtypeStruct(q.shape, BF16),
        scratch_shapes=[pltpu.VMEM((2 * TQ, 1), F32), pltpu.VMEM((2 * TQ, 1), F32),
                        pltpu.VMEM((2 * TQ, V_DIM), F32)],
        compiler_params=pltpu.CompilerParams(dimension_semantics=("parallel", "parallel", "arbitrary"),
                                             vmem_limit_bytes=VMEM_LIMIT),
        name="prompt_attn",
    )(q, kt, v, *lams, g_sub)


def _sample_attn_kernel(pt_ref, q_ref, kn_ref, vn_ref, lq1, lk1, lq2, lk2, gsub_ref, *rest, lam0):
    del pt_ref
    pp = PAGES_PER_STEP
    k_refs, v_refs = rest[:pp], rest[pp:2 * pp]
    o_ref, qh_ref, m_ref, l_ref, acc_ref = rest[2 * pp:]
    g = pl.program_id(1)
    hr = 2 * SUBLANES
    heads = [slice(h * hr, (h + 1) * hr) for h in range(N_HEADS)]
    lanes = [slice(h * V_DIM, (h + 1) * V_DIM) for h in range(N_HEADS)]

    @pl.when(g == 0)
    def _init():
        q = q_ref[...]
        lane = lax.broadcasted_iota(jnp.int32, (SUBLANES, V_DIM), 1)
        for h in range(N_HEADS):
            qh = q[:, lanes[h]]
            qh_ref[heads[h], :] = jnp.concatenate(
                [jnp.where(lane < HEAD_DIM, qh, 0.0), jnp.where(lane >= HEAD_DIM, qh, 0.0)], axis=0).astype(BF16)
        m_ref[...] = jnp.full(m_ref.shape, NEG, F32)
        l_ref[...] = jnp.zeros(l_ref.shape, F32)
        acc_ref[...] = jnp.zeros(acc_ref.shape, F32)

    def update(s, pv):
        m_old = m_ref[...]
        m_new = jnp.maximum(m_old, jnp.max(s, axis=1, keepdims=True))
        corr = jnp.exp(m_old - m_new)
        p = jnp.exp(s - m_new)
        l_ref[...] = l_ref[...] * corr + jnp.sum(p, axis=1, keepdims=True)
        p = p.astype(BF16)
        acc_ref[...] = acc_ref[...] * corr + jnp.concatenate([pv(h, p[heads[h], :]) for h in range(N_HEADS)], axis=0)
        m_ref[...] = m_new

    def page_scores(h):
        kt = jnp.concatenate([r[lanes[h], :].astype(BF16) for r in k_refs], axis=1)
        return jnp.dot(qh_ref[heads[h], :], kt, preferred_element_type=F32)

    def page_pv(h, p):
        vh = jnp.concatenate([r[:, h, :].astype(BF16) for r in v_refs], axis=0)
        return jnp.dot(p, vh, preferred_element_type=F32)

    update(jnp.concatenate([page_scores(h) for h in range(N_HEADS)], axis=0), page_pv)

    @pl.when(g == pl.num_programs(1) - 1)
    def _finish():
        s = jnp.concatenate([_nt_dot(qh_ref[heads[h], :], kn_ref[:, lanes[h]]) for h in range(N_HEADS)], axis=0)
        t = lax.broadcasted_iota(jnp.int32, s.shape, 0) % SUBLANES
        j = lax.broadcasted_iota(jnp.int32, s.shape, 1)
        s = jnp.where(j <= t, s, NEG)
        update(s, lambda h, p: jnp.dot(p, vn_ref[:, lanes[h]], preferred_element_type=F32))
        lam = _diff_lambda(lq1, lk1, lq2, lk2, lam0)
        nrm = acc_ref[...] / l_ref[...]
        for h in range(N_HEADS):
            blk = nrm[heads[h], :]
            o = blk[:SUBLANES] - lam * blk[SUBLANES:]
            o_ref[:, lanes[h]] = _subln(o, gsub_ref[...], lam0)


def _sample_attn(q, k_new, v_new, cache_kt, cache_v, page_table, lams, g_sub, lam0):
    db, n_pages = page_table.shape
    pp = PAGES_PER_STEP
    t = q.shape[0] // db
    rows = N_HEADS * 2 * SUBLANES
    kpage = lambda r: pl.BlockSpec((None, ATT_W, PAGE_SIZE), lambda b, g, pt: (pt[b, g * pp + r], 0, 0))
    vpage = lambda r: pl.BlockSpec((None, PAGE_SIZE, N_HEADS, V_DIM), lambda b, g, pt: (pt[b, g * pp + r], 0, 0, 0))
    small = lambda shape: pl.BlockSpec(shape, lambda b, g, pt: (0, 0))
    new = pl.BlockSpec((None, NEW_PAD, ATT_W), lambda b, g, pt: (b, 0, 0))
    qblk = pl.BlockSpec((t, ATT_W), lambda b, g, pt: (b, 0))
    grid_spec = pltpu.PrefetchScalarGridSpec(
        num_scalar_prefetch=1,
        grid=(db, n_pages // pp),
        in_specs=[qblk, new, new] + [small((1, HEAD_DIM))] * 4 + [small((1, V_DIM))]
                 + [kpage(r) for r in range(pp)] + [vpage(r) for r in range(pp)],
        out_specs=qblk,
        scratch_shapes=[pltpu.VMEM((rows, V_DIM), BF16), pltpu.VMEM((rows, 1), F32), pltpu.VMEM((rows, 1), F32),
                        pltpu.VMEM((rows, V_DIM), F32)],
    )
    return pl.pallas_call(
        functools.partial(_sample_attn_kernel, lam0=lam0),
        grid_spec=grid_spec,
        out_shape=jax.ShapeDtypeStruct(q.shape, F32),
        compiler_params=pltpu.CompilerParams(dimension_semantics=("parallel", "arbitrary"),
                                             vmem_limit_bytes=VMEM_LIMIT),
        name="sample_attn",
    )(page_table, q, k_new, v_new, *lams, g_sub, *([cache_kt] * pp), *([cache_v] * pp))


def _mix_kernel(x_ref, o_ref, u_ref, vn_ref, ga_ref, gb_ref, mixw_ref, mixb_ref, wo_ref, gpost_ref, h_ref):
    vn = vn_ref[...].astype(BF16)
    parts = []
    for g in range(SGU_GROUPS):
        z = jnp.dot(mixw_ref[g], vn[:, g * LANES:(g + 1) * LANES], preferred_element_type=F32)
        parts.append(z + mixb_ref[:, g * LANES:(g + 1) * LANES])
    z = jnp.concatenate(parts, axis=1)
    s_out = u_ref[...].astype(F32) * z
    merged = ga_ref[...].astype(F32) * o_ref[...].astype(F32) + gb_ref[...].astype(F32) * s_out
    y = jnp.dot(merged.astype(BF16), wo_ref[...], preferred_element_type=F32)
    h_ref[...] = x_ref[...] + _rms(y, gpost_ref[...])


def _mix(x, o, u, vn, ga, gb, mixw, mixb, w_o, g_post):
    n = x.shape[0]
    row = lambda i: (i, 0)
    act = pl.BlockSpec((TM, D_MODEL), row)
    return pl.pallas_call(
        _mix_kernel,
        grid=(n // TM,),
        in_specs=[act] * 6 + [_full((SGU_GROUPS, TM, TM)), _full((TM, D_SGU)), _full((D_MODEL, D_MODEL)),
                              _full((1, D_MODEL))],
        out_specs=act,
        out_shape=jax.ShapeDtypeStruct((n, D_MODEL), F32),
        compiler_params=pltpu.CompilerParams(dimension_semantics=("parallel",), vmem_limit_bytes=VMEM_LIMIT),
        name="mix",
    )(x, o, u, vn, ga, gb, mixw, mixb, w_o, g_post)


def _mix_weights(sgu_w, sgu_b, chunk_len):
    wm = (sgu_w * jnp.tril(jnp.ones((CHUNK, CHUNK), sgu_w.dtype)))[:, :chunk_len, :chunk_len]
    eye = jnp.eye(TM // chunk_len, dtype=sgu_w.dtype)
    mixw = jnp.einsum("ab,gij->gaibj", eye, wm).reshape(SGU_GROUPS, TM, TM).astype(BF16)
    bias = jnp.repeat(sgu_b[:, :chunk_len].T, D_SGU // SGU_GROUPS, axis=1)
    return mixw, jnp.tile(bias, (TM // chunk_len, 1))


def _ffn_kernel(h_ref, prev_ref, gpre_ref, wup_ref, cw_ref, cb_ref, wdn_ref, gpost_ref,
                y_ref, a_ref, carry_ref, *, tiles_per_row, sample):
    h = h_ref[...]
    hn = _rms(h, gpre_ref[...]).astype(BF16)
    a = jnp.dot(hn, wup_ref[:, :D_FF], preferred_element_type=F32)
    b = jnp.dot(hn, wup_ref[:, D_FF:], preferred_element_type=F32)
    am1 = pltpu.roll(a, 1, 0)
    am2 = pltpu.roll(a, 2, 0)
    if sample:
        prev = prev_ref[...]
        t = lax.broadcasted_iota(jnp.int32, a.shape, 0) % SUBLANES
        am1 = jnp.where(t == 0, pltpu.roll(prev, TM - 1, 0), am1)
        am2 = jnp.where(t < 2, prev, am2)
        a_ref[...] = a
    else:
        @pl.when(pl.program_id(0) % tiles_per_row == 0)
        def _():
            carry_ref[...] = jnp.zeros(carry_ref.shape, F32)

        last = carry_ref[...]
        t = lax.broadcasted_iota(jnp.int32, last.shape, 0)
        head1 = jnp.where(t == 0, pltpu.roll(last, 1, 0), am1[:SUBLANES])
        head2 = jnp.where(t < 2, pltpu.roll(last, 2, 0), am2[:SUBLANES])
        am1 = jnp.concatenate([head1, am1[SUBLANES:]], axis=0)
        am2 = jnp.concatenate([head2, am2[SUBLANES:]], axis=0)
        carry_ref[...] = a[TM - SUBLANES:]
        a_ref[...] = a[TM - SUBLANES:]
    c = cb_ref[...] + cw_ref[0:1, :] * am2 + cw_ref[1:2, :] * am1 + cw_ref[2:3, :] * a
    gated = (jax.nn.gelu(c) * b).astype(BF16)
    out = jnp.dot(gated, wdn_ref[...], preferred_element_type=F32)
    y_ref[...] = h + _rms(out, gpost_ref[...])


def _ffn(h, prev, g_pre, w_up, conv_w, conv_b, w_down, g_post, *, tiles_per_row, sample):
    n = h.shape[0]
    row = lambda i: (i, 0)
    act = pl.BlockSpec((TM, D_MODEL), row)
    if sample:
        prev_spec = pl.BlockSpec((TM, D_FF), row)
        a_spec = pl.BlockSpec((TM, D_FF), row)
        a_shape = jax.ShapeDtypeStruct((n, D_FF), F32)
    else:
        prev_spec = _full(prev.shape)
        a_spec = pl.BlockSpec((None, SUBLANES, D_FF), lambda i: (i, 0, 0))
        a_shape = jax.ShapeDtypeStruct((n // TM, SUBLANES, D_FF), F32)
    return pl.pallas_call(
        functools.partial(_ffn_kernel, tiles_per_row=tiles_per_row, sample=sample),
        grid=(n // TM,),
        in_specs=[act, prev_spec, _full((1, D_MODEL)), _full((D_MODEL, 2 * D_FF)), _full((CONV_W, D_FF)),
                  _full((1, D_FF)), _full((D_FF, D_MODEL)), _full((1, D_MODEL))],
        out_specs=[act, a_spec],
        out_shape=[jax.ShapeDtypeStruct((n, D_MODEL), F32), a_shape],
        scratch_shapes=[pltpu.VMEM((SUBLANES, D_FF), F32)],
        compiler_params=pltpu.CompilerParams(dimension_semantics=("arbitrary",), vmem_limit_bytes=VMEM_LIMIT),
        name="ffn",
    )(h, prev, g_pre, w_up, conv_w, conv_b, w_down, g_post)


def kernel(x_prompt, x_sample, cache_k, cache_v, state_conv, page_table, w_in, w_o, g_mix_pre, g_mix_post,
           lambda_q1, lambda_k1, lambda_q2, lambda_k2, g_subln, sgu_ln_g, sgu_ln_b, sgu_w, sgu_b, g_ffn_pre,
           g_ffn_post, w_up, conv_w, conv_b, w_down):
    depth = w_in.shape[0]
    assert depth == 1, "one layer per step"
    b, s, _ = x_prompt.shape
    db, t, _ = x_sample.shape
    n_pages = page_table.shape[1]
    past_len = n_pages * PAGE_SIZE
    assert s % TM == 0 and s % TQ == 0 and (db * t) % TM == 0 and t == SUBLANES and TM % CHUNK == 0
    assert n_pages % PAGES_PER_STEP == 0

    lam0 = 0.8 - 0.6 * math.exp(-0.3 * 0)
    lams = (lambda_q1, lambda_k1, lambda_q2, lambda_k2)
    w_in_b, w_o_b = w_in[0].astype(BF16), w_o[0].astype(BF16)
    w_up_b, w_down_b = w_up[0].astype(BF16), w_down[0].astype(BF16)
    proj_w = (w_in_b, g_mix_pre, sgu_ln_g, sgu_ln_b)
    ffn_w = (g_ffn_pre, w_up_b, conv_w[0], conv_b, w_down_b, g_ffn_post)

    xp = x_prompt.reshape(b * s, D_MODEL)
    qp, kpf, kpb, vpf, vpb, up, vnp, gap, gbp = _in_proj(
        xp, *proj_w, _rope_tables(jnp.arange(s)), q_dtype=BF16, vn_dtype=BF16, seq_len=s)
    to3 = lambda a: a.reshape(b, s, ATT_W)
    op = _prompt_attn(to3(qp), kpb, to3(vpb), lams, g_subln, lam0).reshape(b * s, ATT_W)
    k_prompt = kpf.reshape(1, b, N_HEADS, 2, HEAD_DIM, s).transpose(0, 1, 5, 2, 3, 4)
    mixw_p, mixb_p = _mix_weights(sgu_w[0], sgu_b[0], CHUNK)
    hp = _mix(xp, op, up, vnp, gap, gbp, mixw_p, mixb_p, w_o_b, g_mix_post)
    yp, ap = _ffn(hp, jnp.zeros((SUBLANES, LANES), F32), *ffn_w, tiles_per_row=s // TM, sample=False)
    conv_p = ap.reshape(b, s // TM, SUBLANES, D_FF)[:, -1, SUBLANES - (CONV_W - 1):]

    xs = x_sample.reshape(db * t, D_MODEL)
    pos_s = jnp.tile(past_len + jnp.arange(t), TM // t)
    qs, ksf, ksb, vsf, vsb, us, vns, gas, gbs = _in_proj(
        xs, *proj_w, _rope_tables(pos_s), q_dtype=F32, vn_dtype=F32)
    pad_new = lambda a: jnp.pad(a.reshape(db, t, ATT_W), ((0, 0), (0, NEW_PAD - t), (0, 0)))
    ckt = cache_k[0].transpose(0, 2, 3, 4, 1).reshape(cache_k.shape[1], ATT_W, PAGE_SIZE)
    cv = cache_v[0]
    os_ = _sample_attn(qs, pad_new(ksb), pad_new(vsb), ckt, cv, page_table, lams, g_subln, lam0)
    mixw_s, mixb_s = _mix_weights(sgu_w[0], sgu_b[0], t)
    hs = _mix(xs, os_, us, vns, gas, gbs, mixw_s, mixb_s, w_o_b, g_mix_post)
    prev = jnp.pad(state_conv[0], ((0, 0), (0, t - (CONV_W - 1)), (0, 0))).reshape(db * t, D_FF)
    ys, as_ = _ffn(hs, prev, *ffn_w, tiles_per_row=1, sample=True)
    conv_s = as_.reshape(db, t, D_FF)[:, t - (CONV_W - 1):]

    return (yp.reshape(b, s, D_MODEL), ys.reshape(db, t, D_MODEL),
            k_prompt, vpf.reshape(1, b, s, N_HEADS, V_DIM),
            conv_p[None], ksf.reshape(1, db, t, N_HEADS, 2, HEAD_DIM), vsf.reshape(1, db, t, N_HEADS, V_DIM),
            vns.reshape(1, db, t, D_SGU), conv_s[None])
```

```python
import functools
import math

import jax
import jax.numpy as jnp
from jax import lax
from jax.experimental import pallas as pl
from jax.experimental.pallas import tpu as pltpu

F32 = jnp.float32
BF16 = jnp.bfloat16

D_MODEL = 1024
N_HEADS = 8
HEAD_DIM = 64
V_DIM = 2 * HEAD_DIM
ATT_W = N_HEADS * 2 * HEAD_DIM
ROT_DIM = HEAD_DIM // 4
ROPE_THETA = 500000.0
SGU_GROUPS = 8
CHUNK = 128
D_SGU = 1024
D_FF = 2816
CONV_W = 3
EPS = 1e-6
PAGE_SIZE = 128
D_IN = 3 * ATT_W + 2 * D_SGU + 2 * D_MODEL
NEG = -1e30

LANES = 128
SUBLANES = 8
VMEM_LIMIT = 56 * 1024 * 1024

TM = 256
TQ = 256
HEADS_PER_STEP = 2
PAGES_PER_STEP = 8
NEW_PAD = 16


def _rms(x, g):
    return x * lax.rsqrt(jnp.mean(x * x, axis=-1, keepdims=True) + EPS) * g


def _nt_dot(a, b):
    return lax.dot_general(a, b, (((1,), (1,)), ((), ())), preferred_element_type=F32)


def _full(shape):
    return pl.BlockSpec(shape, lambda i: (0,) * len(shape), pipeline_mode=pl.Buffered(1))


def _in_proj_kernel(x_ref, w_ref, gpre_ref, lng_ref, lnb_ref, cos_ref, sa_ref, sb_ref,
                    q_ref, kf_ref, kb_ref, vf_ref, vb_ref, u_ref, vn_ref, ga_ref, gb_ref, *, transposed):
    xn = _rms(x_ref[...], gpre_ref[...]).astype(BF16)
    heads = [slice(h * V_DIM, (h + 1) * V_DIM) for h in range(N_HEADS)]

    def proj(seg):
        return jnp.dot(xn, w_ref[:, seg * ATT_W:(seg + 1) * ATT_W], preferred_element_type=F32)

    cos, sa, sb = cos_ref[...], sa_ref[...], sb_ref[...]

    def rope(p, h):
        s = p[:, heads[h]]
        return s * cos + pltpu.roll(s, LANES - ROT_DIM // 2, 1) * sa + pltpu.roll(s, ROT_DIM // 2, 1) * sb

    p = proj(0)
    for h in range(N_HEADS):
        r = rope(p, h) * (HEAD_DIM ** -0.5)
        if transposed:
            q_ref[heads[h], :] = r.T.astype(q_ref.dtype)
        else:
            q_ref[:, heads[h]] = r.astype(q_ref.dtype)
    p = proj(1)
    for h in range(N_HEADS):
        r = rope(p, h)
        if transposed:
            kf_ref[heads[h], :] = r.T
        else:
            kf_ref[:, heads[h]] = r
        kb_ref[:, heads[h]] = r.astype(BF16)
    p = proj(2)
    vf_ref[...] = p
    if transposed:
        for h in range(N_HEADS):
            vb_ref[heads[h], :] = p[:, heads[h]].T.astype(BF16)
    else:
        vb_ref[...] = p.astype(BF16)
    u_ref[...] = jax.nn.gelu(proj(3)).astype(BF16)
    g = jax.nn.gelu(proj(4))
    mu = jnp.mean(g, axis=-1, keepdims=True)
    gc = g - mu
    vn = gc * lax.rsqrt(jnp.mean(gc * gc, axis=-1, keepdims=True) + EPS) * lng_ref[...] + lnb_ref[...]
    vn_ref[...] = vn.astype(vn_ref.dtype)
    ga_ref[...] = jax.nn.sigmoid(proj(5)).astype(BF16)
    gb_ref[...] = jax.nn.sigmoid(proj(6)).astype(BF16)


def _in_proj(x, w_in, g_pre, ln_g, ln_b, tabs, *, q_dtype, vn_dtype, seq_len=None):
    n = x.shape[0]
    n_tab = tabs[0].shape[0] // TM
    row = lambda i: (i, 0)
    tab = lambda i: (i % n_tab, 0)
    act = pl.BlockSpec((TM, D_MODEL), row)
    outs = [(q_dtype, ATT_W), (F32, ATT_W), (BF16, ATT_W), (F32, ATT_W), (BF16, ATT_W),
            (BF16, D_SGU), (vn_dtype, D_SGU), (BF16, D_MODEL), (BF16, D_MODEL)]
    out_specs = [pl.BlockSpec((TM, w), row) for _, w in outs]
    out_shape = [jax.ShapeDtypeStruct((n, w), d) for d, w in outs]
    if seq_len is not None:
        tpr = seq_len // TM
        for idx in (0, 1, 4):
            out_specs[idx] = pl.BlockSpec((None, ATT_W, TM), lambda i: (i // tpr, 0, i % tpr))
            out_shape[idx] = jax.ShapeDtypeStruct((n // seq_len, ATT_W, seq_len), outs[idx][0])
    return pl.pallas_call(
        functools.partial(_in_proj_kernel, transposed=seq_len is not None),
        grid=(n // TM,),
        in_specs=[act, _full((D_MODEL, D_IN)), _full((1, D_MODEL)), _full((1, D_SGU)), _full((1, D_SGU)),
                  pl.BlockSpec((TM, LANES), tab), pl.BlockSpec((TM, LANES), tab), pl.BlockSpec((TM, LANES), tab)],
        out_specs=out_specs,
        out_shape=out_shape,
        compiler_params=pltpu.CompilerParams(dimension_semantics=("parallel",), vmem_limit_bytes=VMEM_LIMIT),
        name="in_proj",
    )(x, w_in, g_pre, ln_g, ln_b, *tabs)


def _rope_tables(pos):
    half = ROT_DIM // 2
    inv = ROPE_THETA ** (-jnp.arange(0, ROT_DIM, 2, dtype=F32) / ROT_DIM)
    ang = pos.astype(F32)[:, None] * inv[None, :]
    cos, sin = jnp.cos(ang), jnp.sin(ang)
    d = jnp.arange(LANES) % HEAD_DIM
    lo, hi = d < half, (d >= half) & (d < ROT_DIM)
    idx = d % half
    c = jnp.where((lo | hi)[None, :], cos[:, idx], 1.0)
    sa = jnp.where(lo[None, :], -sin[:, idx], 0.0)
    sb = jnp.where(hi[None, :], sin[:, idx], 0.0)
    return c, sa, sb


def _diff_lambda(lq1, lk1, lq2, lk2, lam0):
    s1 = jnp.sum(lq1[...] * lk1[...], axis=-1, keepdims=True)
    s2 = jnp.sum(lq2[...] * lk2[...], axis=-1, keepdims=True)
    return jnp.exp(s1) - jnp.exp(s2) + lam0


def _subln(o, g, lam0):
    return _rms(o, g) * (1.0 - lam0)


def _prompt_attn_kernel(qt_ref, k_ref, vt_ref, lq1, lk1, lq2, lk2, gsub_ref, o_ref, m_ref, l_ref, acc_ref, s_ref,
                        *, lam0):
    hps = HEADS_PER_STEP
    qi = pl.program_id(2)
    feat = lax.broadcasted_iota(jnp.int32, (V_DIM, TQ), 0)
    hsl = [slice(hh * V_DIM, (hh + 1) * V_DIM) for hh in range(hps)]
    qst = []
    for hh in range(hps):
        qt = qt_ref[hsl[hh], :]
        zero = jnp.zeros_like(qt)
        qst.append(jnp.concatenate([jnp.where(feat < HEAD_DIM, qt, zero), jnp.where(feat >= HEAD_DIM, qt, zero)],
                                   axis=1))
    m_ref[...] = jnp.full(m_ref.shape, NEG, F32)
    l_ref[...] = jnp.zeros(l_ref.shape, F32)
    acc_ref[...] = jnp.zeros(acc_ref.shape, F32)

    def scores(kv, slot):
        start = pl.multiple_of(kv * TQ, TQ)
        for hh in range(hps):
            s_ref[slot, hh] = jnp.dot(k_ref[pl.ds(start, TQ), hsl[hh]], qst[hh], preferred_element_type=F32)

    def softmax_pv(kv, slot, masked):
        start = pl.multiple_of(kv * TQ, TQ)
        ps, corrs = [], []
        for hh in range(hps):
            s = s_ref[slot, hh]
            if masked:
                key = lax.broadcasted_iota(jnp.int32, s.shape, 0)
                qry = lax.broadcasted_iota(jnp.int32, s.shape, 1) % TQ
                s = jnp.where(key <= qry, s, NEG)
            m_old = m_ref[hh]
            m_new = jnp.maximum(m_old, jnp.max(s, axis=0, keepdims=True))
            corr = jnp.exp(m_old - m_new)
            p = jnp.exp(s - m_new)
            l_ref[hh] = l_ref[hh] * corr + jnp.sum(p, axis=0, keepdims=True)
            m_ref[hh] = m_new
            ps.append(p.astype(BF16))
            corrs.append(corr)
        for hh in range(hps):
            acc_ref[hh] = acc_ref[hh] * corrs[hh] + jnp.dot(vt_ref[hsl[hh], pl.ds(start, TQ)], ps[hh],
                                                             preferred_element_type=F32)

    scores(0, 0)

    def pair(j, carry):
        kv = 2 * j
        scores(kv + 1, 1)
        softmax_pv(kv, 0, False)
        scores(kv + 2, 0)
        softmax_pv(kv + 1, 1, False)
        return carry

    lax.fori_loop(0, qi // 2, pair, 0)

    @pl.when(qi % 2 == 1)
    def _odd():
        scores(qi, 1)
        softmax_pv(qi - 1, 0, False)
        softmax_pv(qi, 1, True)

    @pl.when(qi % 2 == 0)
    def _even():
        softmax_pv(qi, 0, True)

    lam = _diff_lambda(lq1, lk1, lq2, lk2, lam0)
    for hh in range(hps):
        nrm = acc_ref[hh] / l_ref[hh]
        ot = nrm[:, :TQ] - lam * nrm[:, TQ:]
        o_ref[:, hsl[hh]] = _subln(ot.T, gsub_ref[...], lam0).astype(o_ref.dtype)


def _prompt_attn(qt, k, vt, lams, g_sub, lam0):
    b, s, _ = k.shape
    hps = HEADS_PER_STEP
    w = hps * V_DIM
    qblk = pl.BlockSpec((None, w, TQ), lambda bi, h, qi: (bi, h, qi))
    vblk = pl.BlockSpec((None, w, s), lambda bi, h, qi: (bi, h, 0))
    kblk = pl.BlockSpec((None, s, w), lambda bi, h, qi: (bi, 0, h))
    oblk = pl.BlockSpec((None, TQ, w), lambda bi, h, qi: (bi, qi, h))
    small = lambda shape: pl.BlockSpec(shape, lambda bi, h, qi: (0, 0))
    return pl.pallas_call(
        functools.partial(_prompt_attn_kernel, lam0=lam0),
        grid=(b, N_HEADS // hps, s // TQ),
        in_specs=[qblk, kblk, vblk] + [small((1, HEAD_DIM))] * 4 + [small((1, V_DIM))],
        out_specs=oblk,
        out_shape=jax.ShapeDtypeStruct((b, s, ATT_W), BF16),
        scratch_shapes=[pltpu.VMEM((hps, 1, 2 * TQ), F32), pltpu.VMEM((hps, 1, 2 * TQ), F32),
                        pltpu.VMEM((hps, V_DIM, 2 * TQ), F32), pltpu.VMEM((2, hps, TQ, 2 * TQ), F32)],
        compiler_params=pltpu.CompilerParams(dimension_semantics=("parallel", "parallel", "arbitrary"),
                                             vmem_limit_bytes=VMEM_LIMIT),
        name="prompt_attn",
    )(qt, k, vt, *lams, g_sub)


def _sample_attn_kernel(pt_ref, q_ref, kn_ref, vn_ref, lq1, lk1, lq2, lk2, gsub_ref, *rest, lam0):
    del pt_ref
    pp = PAGES_PER_STEP
    k_refs, v_refs = rest[:pp], rest[pp:2 * pp]
    o_ref, qh_ref, m_ref, l_ref, acc_ref = rest[2 * pp:]
    g = pl.program_id(1)
    hr = 2 * SUBLANES
    heads = [slice(h * hr, (h + 1) * hr) for h in range(N_HEADS)]
    lanes = [slice(h * V_DIM, (h + 1) * V_DIM) for h in range(N_HEADS)]

    @pl.when(g == 0)
    def _init():
        q = q_ref[...]
        lane = lax.broadcasted_iota(jnp.int32, (SUBLANES, V_DIM), 1)
        for h in range(N_HEADS):
            qh = q[:, lanes[h]]
            qh_ref[heads[h], :] = jnp.concatenate(
                [jnp.where(lane < HEAD_DIM, qh, 0.0), jnp.where(lane >= HEAD_DIM, qh, 0.0)], axis=0).astype(BF16)
        m_ref[...] = jnp.full(m_ref.shape, NEG, F32)
        l_ref[...] = jnp.zeros(l_ref.shape, F32)
        acc_ref[...] = jnp.zeros(acc_ref.shape, F32)

    def update(s, pv):
        m_old = m_ref[...]
        m_new = jnp.maximum(m_old, jnp.max(s, axis=1, keepdims=True))
        corr = jnp.exp(m_old - m_new)
        p = jnp.exp(s - m_new)
        l_ref[...] = l_ref[...] * corr + jnp.sum(p, axis=1, keepdims=True)
        p = p.astype(BF16)
        acc_ref[...] = acc_ref[...] * corr + jnp.concatenate([pv(h, p[heads[h], :]) for h in range(N_HEADS)], axis=0)
        m_ref[...] = m_new

    def page_scores(h):
        kt = jnp.concatenate([r[lanes[h], :].astype(BF16) for r in k_refs], axis=1)
        return jnp.dot(qh_ref[heads[h], :], kt, preferred_element_type=F32)

    def page_pv(h, p):
        vh = jnp.concatenate([r[pl.ds(h, PAGE_SIZE, stride=N_HEADS), :].astype(BF16) for r in v_refs], axis=0)
        return jnp.dot(p, vh, preferred_element_type=F32)

    update(jnp.concatenate([page_scores(h) for h in range(N_HEADS)], axis=0), page_pv)

    @pl.when(g == pl.num_programs(1) - 1)
    def _finish():
        s = jnp.concatenate([_nt_dot(qh_ref[heads[h], :], kn_ref[:, lanes[h]]) for h in range(N_HEADS)], axis=0)
        t = lax.broadcasted_iota(jnp.int32, s.shape, 0) % SUBLANES
        j = lax.broadcasted_iota(jnp.int32, s.shape, 1)
        s = jnp.where(j <= t, s, NEG)
        update(s, lambda h, p: jnp.dot(p, vn_ref[:, lanes[h]], preferred_element_type=F32))
        lam = _diff_lambda(lq1, lk1, lq2, lk2, lam0)
        nrm = acc_ref[...] / l_ref[...]
        for h in range(N_HEADS):
            blk = nrm[heads[h], :]
            o = blk[:SUBLANES] - lam * blk[SUBLANES:]
            o_ref[:, lanes[h]] = _subln(o, gsub_ref[...], lam0)


def _sample_attn(q, k_new, v_new, cache_kt, cache_v2, page_table, lams, g_sub, lam0):
    db, n_pages = page_table.shape
    pp = PAGES_PER_STEP
    t = q.shape[0] // db
    rows = N_HEADS * 2 * SUBLANES
    kpage = lambda r: pl.BlockSpec((None, ATT_W, PAGE_SIZE), lambda b, g, pt: (pt[b, g * pp + r], 0, 0))
    vpage = lambda r: pl.BlockSpec((None, PAGE_SIZE * N_HEADS, V_DIM), lambda b, g, pt: (pt[b, g * pp + r], 0, 0))
    small = lambda shape: pl.BlockSpec(shape, lambda b, g, pt: (0, 0))
    new = pl.BlockSpec((None, NEW_PAD, ATT_W), lambda b, g, pt: (b, 0, 0))
    qblk = pl.BlockSpec((t, ATT_W), lambda b, g, pt: (b, 0))
    grid_spec = pltpu.PrefetchScalarGridSpec(
        num_scalar_prefetch=1,
        grid=(db, n_pages // pp),
        in_specs=[qblk, new, new] + [small((1, HEAD_DIM))] * 4 + [small((1, V_DIM))]
                 + [kpage(r) for r in range(pp)] + [vpage(r) for r in range(pp)],
        out_specs=qblk,
        scratch_shapes=[pltpu.VMEM((rows, V_DIM), BF16), pltpu.VMEM((rows, 1), F32), pltpu.VMEM((rows, 1), F32),
                        pltpu.VMEM((rows, V_DIM), F32)],
    )
    return pl.pallas_call(
        functools.partial(_sample_attn_kernel, lam0=lam0),
        grid_spec=grid_spec,
        out_shape=jax.ShapeDtypeStruct(q.shape, F32),
        compiler_params=pltpu.CompilerParams(dimension_semantics=("parallel", "arbitrary"),
                                             vmem_limit_bytes=VMEM_LIMIT),
        name="sample_attn",
    )(page_table, q, k_new, v_new, *lams, g_sub, *([cache_kt] * pp), *([cache_v2] * pp))


def _mix_kernel(x_ref, o_ref, u_ref, vn_ref, ga_ref, gb_ref, mixw_ref, mixb_ref, wo_ref, gpost_ref, h_ref):
    vn = vn_ref[...].astype(BF16)
    parts = []
    for g in range(SGU_GROUPS):
        z = jnp.dot(mixw_ref[g], vn[:, g * LANES:(g + 1) * LANES], preferred_element_type=F32)
        parts.append(z + mixb_ref[:, g * LANES:(g + 1) * LANES])
    z = jnp.concatenate(parts, axis=1)
    s_out = u_ref[...].astype(F32) * z
    merged = ga_ref[...].astype(F32) * o_ref[...].astype(F32) + gb_ref[...].astype(F32) * s_out
    y = jnp.dot(merged.astype(BF16), wo_ref[...], preferred_element_type=F32)
    h_ref[...] = x_ref[...] + _rms(y, gpost_ref[...])


def _mix(x, o, u, vn, ga, gb, mixw, mixb, w_o, g_post):
    n = x.shape[0]
    row = lambda i: (i, 0)
    act = pl.BlockSpec((TM, D_MODEL), row)
    return pl.pallas_call(
        _mix_kernel,
        grid=(n // TM,),
        in_specs=[act] * 6 + [_full((SGU_GROUPS, TM, TM)), _full((TM, D_SGU)), _full((D_MODEL, D_MODEL)),
                              _full((1, D_MODEL))],
        out_specs=act,
        out_shape=jax.ShapeDtypeStruct((n, D_MODEL), F32),
        compiler_params=pltpu.CompilerParams(dimension_semantics=("parallel",), vmem_limit_bytes=VMEM_LIMIT),
        name="mix",
    )(x, o, u, vn, ga, gb, mixw, mixb, w_o, g_post)


def _mix_weights(sgu_w, sgu_b, chunk_len):
    wm = (sgu_w * jnp.tril(jnp.ones((CHUNK, CHUNK), sgu_w.dtype)))[:, :chunk_len, :chunk_len]
    eye = jnp.eye(TM // chunk_len, dtype=sgu_w.dtype)
    mixw = jnp.einsum("ab,gij->gaibj", eye, wm).reshape(SGU_GROUPS, TM, TM).astype(BF16)
    bias = jnp.repeat(sgu_b[:, :chunk_len].T, D_SGU // SGU_GROUPS, axis=1)
    return mixw, jnp.tile(bias, (TM // chunk_len, 1))


def _ffn_kernel(h_ref, prev_ref, gpre_ref, wup_ref, cw_ref, cb_ref, wdn_ref, gpost_ref,
                y_ref, a_ref, carry_ref, *, tiles_per_row, sample):
    h = h_ref[...]
    hn = _rms(h, gpre_ref[...]).astype(BF16)
    a = jnp.dot(hn, wup_ref[:, :D_FF], preferred_element_type=F32)
    b = jnp.dot(hn, wup_ref[:, D_FF:], preferred_element_type=F32)
    am1 = pltpu.roll(a, 1, 0)
    am2 = pltpu.roll(a, 2, 0)
    if sample:
        prev = prev_ref[...]
        t = lax.broadcasted_iota(jnp.int32, a.shape, 0) % SUBLANES
        am1 = jnp.where(t == 0, pltpu.roll(prev, TM - 1, 0), am1)
        am2 = jnp.where(t < 2, prev, am2)
        a_ref[...] = a
    else:
        @pl.when(pl.program_id(0) % tiles_per_row == 0)
        def _():
            carry_ref[...] = jnp.zeros(carry_ref.shape, F32)

        last = carry_ref[...]
        t = lax.broadcasted_iota(jnp.int32, last.shape, 0)
        head1 = jnp.where(t == 0, pltpu.roll(last, 1, 0), am1[:SUBLANES])
        head2 = jnp.where(t < 2, pltpu.roll(last, 2, 0), am2[:SUBLANES])
        am1 = jnp.concatenate([head1, am1[SUBLANES:]], axis=0)
        am2 = jnp.concatenate([head2, am2[SUBLANES:]], axis=0)
        carry_ref[...] = a[TM - SUBLANES:]
        a_ref[...] = a[TM - SUBLANES:]
    c = cb_ref[...] + cw_ref[0:1, :] * am2 + cw_ref[1:2, :] * am1 + cw_ref[2:3, :] * a
    gated = (jax.nn.gelu(c) * b).astype(BF16)
    out = jnp.dot(gated, wdn_ref[...], preferred_element_type=F32)
    y_ref[...] = h + _rms(out, gpost_ref[...])


def _ffn(h, prev, g_pre, w_up, conv_w, conv_b, w_down, g_post, *, tiles_per_row, sample):
    n = h.shape[0]
    row = lambda i: (i, 0)
    act = pl.BlockSpec((TM, D_MODEL), row)
    if sample:
        prev_spec = pl.BlockSpec((TM, D_FF), row)
        a_spec = pl.BlockSpec((TM, D_FF), row)
        a_shape = jax.ShapeDtypeStruct((n, D_FF), F32)
    else:
        prev_spec = _full(prev.shape)
        a_spec = pl.BlockSpec((None, SUBLANES, D_FF), lambda i: (i, 0, 0))
        a_shape = jax.ShapeDtypeStruct((n // TM, SUBLANES, D_FF), F32)
    return pl.pallas_call(
        functools.partial(_ffn_kernel, tiles_per_row=tiles_per_row, sample=sample),
        grid=(n // TM,),
        in_specs=[act, prev_spec, _full((1, D_MODEL)), _full((D_MODEL, 2 * D_FF)), _full((CONV_W, D_FF)),
                  _full((1, D_FF)), _full((D_FF, D_MODEL)), _full((1, D_MODEL))],
        out_specs=[act, a_spec],
        out_shape=[jax.ShapeDtypeStruct((n, D_MODEL), F32), a_shape],
        scratch_shapes=[pltpu.VMEM((SUBLANES, D_FF), F32)],
        compiler_params=pltpu.CompilerParams(dimension_semantics=("arbitrary",), vmem_limit_bytes=VMEM_LIMIT),
        name="ffn",
    )(h, prev, g_pre, w_up, conv_w, conv_b, w_down, g_post)


def kernel(x_prompt, x_sample, cache_k, cache_v, state_conv, page_table, w_in, w_o, g_mix_pre, g_mix_post,
           lambda_q1, lambda_k1, lambda_q2, lambda_k2, g_subln, sgu_ln_g, sgu_ln_b, sgu_w, sgu_b, g_ffn_pre,
           g_ffn_post, w_up, conv_w, conv_b, w_down):
    depth = w_in.shape[0]
    assert depth == 1, "one layer per step"
    b, s, _ = x_prompt.shape
    db, t, _ = x_sample.shape
    n_pool, n_pages = cache_k.shape[1], page_table.shape[1]
    past_len = n_pages * PAGE_SIZE
    assert s % TM == 0 and s % TQ == 0 and (db * t) % TM == 0 and t == SUBLANES and TM % CHUNK == 0
    assert n_pages % PAGES_PER_STEP == 0 and N_HEADS % HEADS_PER_STEP == 0

    lam0 = 0.8 - 0.6 * math.exp(-0.3 * 0)
    lams = (lambda_q1, lambda_k1, lambda_q2, lambda_k2)
    w_in_b, w_o_b = w_in[0].astype(BF16), w_o[0].astype(BF16)
    w_up_b, w_down_b = w_up[0].astype(BF16), w_down[0].astype(BF16)
    proj_w = (w_in_b, g_mix_pre, sgu_ln_g, sgu_ln_b)
    ffn_w = (g_ffn_pre, w_up_b, conv_w[0], conv_b, w_down_b, g_ffn_post)

    xp = x_prompt.reshape(b * s, D_MODEL)
    qpt, kpt, kpb, vpf, vpt, up, vnp, gap, gbp = _in_proj(
        xp, *proj_w, _rope_tables(jnp.arange(s)), q_dtype=BF16, vn_dtype=BF16, seq_len=s)
    op = _prompt_attn(qpt, kpb.reshape(b, s, ATT_W), vpt, lams, g_subln, lam0).reshape(b * s, ATT_W)
    k_prompt = kpt.reshape(1, b, N_HEADS, 2, HEAD_DIM, s).transpose(0, 1, 5, 2, 3, 4)
    mixw_p, mixb_p = _mix_weights(sgu_w[0], sgu_b[0], CHUNK)
    hp = _mix(xp, op, up, vnp, gap, gbp, mixw_p, mixb_p, w_o_b, g_mix_post)
    yp, ap = _ffn(hp, jnp.zeros((SUBLANES, LANES), F32), *ffn_w, tiles_per_row=s // TM, sample=False)
    conv_p = ap.reshape(b, s // TM, SUBLANES, D_FF)[:, -1, SUBLANES - (CONV_W - 1):]

    xs = x_sample.reshape(db * t, D_MODEL)
    pos_s = jnp.tile(past_len + jnp.arange(t), TM // t)
    qs, ksf, ksb, vsf, vsb, us, vns, gas, gbs = _in_proj(
        xs, *proj_w, _rope_tables(pos_s), q_dtype=F32, vn_dtype=F32)
    pad_new = lambda a: jnp.pad(a.reshape(db, t, ATT_W), ((0, 0), (0, NEW_PAD - t), (0, 0)))
    ckt = cache_k[0].transpose(0, 2, 3, 4, 1).reshape(n_pool, ATT_W, PAGE_SIZE)
    cv2 = cache_v[0].reshape(n_pool, PAGE_SIZE * N_HEADS, V_DIM)
    os_ = _sample_attn(qs, pad_new(ksb), pad_new(vsb), ckt, cv2, page_table, lams, g_subln, lam0)
    mixw_s, mixb_s = _mix_weights(sgu_w[0], sgu_b[0], t)
    hs = _mix(xs, os_, us, vns, gas, gbs, mixw_s, mixb_s, w_o_b, g_mix_post)
    prev = jnp.pad(state_conv[0], ((0, 0), (0, t - (CONV_W - 1)), (0, 0))).reshape(db * t, D_FF)
    ys, as_ = _ffn(hs, prev, *ffn_w, tiles_per_row=1, sample=True)
    conv_s = as_.reshape(db, t, D_FF)[:, t - (CONV_W - 1):]

    return (yp.reshape(b, s, D_MODEL), ys.reshape(db, t, D_MODEL),
            k_prompt, vpf.reshape(1, b, s, N_HEADS, V_DIM),
            conv_p[None], ksf.reshape(1, db, t, N_HEADS, 2, HEAD_DIM), vsf.reshape(1, db, t, N_HEADS, V_DIM),
            vns.reshape(1, db, t, D_SGU), conv_s[None])
```

```python
import functools
import math

import jax
import jax.numpy as jnp
from jax import lax
from jax.experimental import pallas as pl
from jax.experimental.pallas import tpu as pltpu

F32 = jnp.float32
BF16 = jnp.bfloat16

D_MODEL = 1024
N_HEADS = 8
HEAD_DIM = 64
V_DIM = 2 * HEAD_DIM
ATT_W = N_HEADS * 2 * HEAD_DIM
ROT_DIM = HEAD_DIM // 4
ROPE_THETA = 500000.0
SGU_GROUPS = 8
CHUNK = 128
D_SGU = 1024
D_FF = 2816
CONV_W = 3
EPS = 1e-6
PAGE_SIZE = 128
D_IN = 3 * ATT_W + 2 * D_SGU + 2 * D_MODEL
NEG = -1e30

LANES = 128
SUBLANES = 8
VMEM_LIMIT = 56 * 1024 * 1024

TM = 256
TQ = 256
HEADS_PER_STEP = 4
PAGES_PER_STEP = 8
NEW_PAD = 16
ONES_ROWS = 16
FF_CHUNKS = ((0, 1536), (1536, D_FF))
LOG2E = 1.4426950408889634


def _rms(x, g):
    return x * lax.rsqrt(jnp.mean(x * x, axis=-1, keepdims=True) + EPS) * g


def _nt_dot(a, b):
    return lax.dot_general(a, b, (((1,), (1,)), ((), ())), preferred_element_type=F32)


def _full(shape):
    return pl.BlockSpec(shape, lambda i: (0,) * len(shape), pipeline_mode=pl.Buffered(1))


def _in_proj_kernel(x_ref, w_ref, gpre_ref, lng_ref, lnb_ref, cos_ref, sa_ref, sb_ref,
                    q_ref, kf_ref, kb_ref, vf_ref, vb_ref, u_ref, vn_ref, ga_ref, gb_ref, *, transposed):
    xn = _rms(x_ref[...], gpre_ref[...]).astype(BF16)
    q_scale = HEAD_DIM ** -0.5 * (LOG2E if transposed else 1.0)
    heads = [slice(h * V_DIM, (h + 1) * V_DIM) for h in range(N_HEADS)]

    def proj(seg):
        return jnp.dot(xn, w_ref[:, seg * ATT_W:(seg + 1) * ATT_W], preferred_element_type=F32)

    cos, sa, sb = cos_ref[...], sa_ref[...], sb_ref[...]

    def rope(p, h):
        s = p[:, heads[h]]
        return s * cos + pltpu.roll(s, LANES - ROT_DIM // 2, 1) * sa + pltpu.roll(s, ROT_DIM // 2, 1) * sb

    p = proj(0)
    for h in range(N_HEADS):
        r = rope(p, h) * q_scale
        if transposed:
            q_ref[heads[h], :] = r.T.astype(q_ref.dtype)
        else:
            q_ref[:, heads[h]] = r.astype(q_ref.dtype)
    p = proj(1)
    for h in range(N_HEADS):
        r = rope(p, h)
        if transposed:
            kf_ref[heads[h], :] = r.T
        else:
            kf_ref[:, heads[h]] = r
        kb_ref[:, heads[h]] = r.astype(BF16)
    p = proj(2)
    vf_ref[...] = p
    if transposed:
        for h in range(N_HEADS):
            vb_ref[heads[h], :] = p[:, heads[h]].T.astype(BF16)
    else:
        vb_ref[...] = p.astype(BF16)
    u_ref[...] = jax.nn.gelu(proj(3)).astype(BF16)
    g = jax.nn.gelu(proj(4))
    mu = jnp.mean(g, axis=-1, keepdims=True)
    gc = g - mu
    vn = gc * lax.rsqrt(jnp.mean(gc * gc, axis=-1, keepdims=True) + EPS) * lng_ref[...] + lnb_ref[...]
    vn_ref[...] = vn.astype(vn_ref.dtype)
    ga_ref[...] = jax.nn.sigmoid(proj(5)).astype(BF16)
    gb_ref[...] = jax.nn.sigmoid(proj(6)).astype(BF16)


def _in_proj(x, w_in, g_pre, ln_g, ln_b, tabs, *, q_dtype, vn_dtype, seq_len=None):
    n = x.shape[0]
    n_tab = tabs[0].shape[0] // TM
    row = lambda i: (i, 0)
    tab = lambda i: (i % n_tab, 0)
    act = pl.BlockSpec((TM, D_MODEL), row)
    outs = [(q_dtype, ATT_W), (F32, ATT_W), (BF16, ATT_W), (F32, ATT_W), (BF16, ATT_W),
            (BF16, D_SGU), (vn_dtype, D_SGU), (BF16, D_MODEL), (BF16, D_MODEL)]
    out_specs = [pl.BlockSpec((TM, w), row) for _, w in outs]
    out_shape = [jax.ShapeDtypeStruct((n, w), d) for d, w in outs]
    if seq_len is not None:
        tpr = seq_len // TM
        for idx in (0, 1, 4):
            out_specs[idx] = pl.BlockSpec((None, ATT_W, TM), lambda i: (i // tpr, 0, i % tpr))
            out_shape[idx] = jax.ShapeDtypeStruct((n // seq_len, ATT_W, seq_len), outs[idx][0])
    return pl.pallas_call(
        functools.partial(_in_proj_kernel, transposed=seq_len is not None),
        grid=(n // TM,),
        in_specs=[act, _full((D_MODEL, D_IN)), _full((1, D_MODEL)), _full((1, D_SGU)), _full((1, D_SGU)),
                  pl.BlockSpec((TM, LANES), tab), pl.BlockSpec((TM, LANES), tab), pl.BlockSpec((TM, LANES), tab)],
        out_specs=out_specs,
        out_shape=out_shape,
        compiler_params=pltpu.CompilerParams(dimension_semantics=("parallel",), vmem_limit_bytes=VMEM_LIMIT),
        name="in_proj",
    )(x, w_in, g_pre, ln_g, ln_b, *tabs)


def _rope_tables(pos):
    half = ROT_DIM // 2
    inv = ROPE_THETA ** (-jnp.arange(0, ROT_DIM, 2, dtype=F32) / ROT_DIM)
    ang = pos.astype(F32)[:, None] * inv[None, :]
    cos, sin = jnp.cos(ang), jnp.sin(ang)
    d = jnp.arange(LANES) % HEAD_DIM
    lo, hi = d < half, (d >= half) & (d < ROT_DIM)
    idx = d % half
    c = jnp.where((lo | hi)[None, :], cos[:, idx], 1.0)
    sa = jnp.where(lo[None, :], -sin[:, idx], 0.0)
    sb = jnp.where(hi[None, :], sin[:, idx], 0.0)
    return c, sa, sb


def _diff_lambda(lq1, lk1, lq2, lk2, lam0):
    s1 = jnp.sum(lq1[...] * lk1[...], axis=-1, keepdims=True)
    s2 = jnp.sum(lq2[...] * lk2[...], axis=-1, keepdims=True)
    return jnp.exp(s1) - jnp.exp(s2) + lam0


def _subln(o, g, lam0):
    return _rms(o, g) * (1.0 - lam0)


def _prompt_attn_kernel(qt_ref, k_ref, vt_ref, lq1, lk1, lq2, lk2, gsub_ref, o_ref, m_ref, acc_ref, s_ref, *, lam0):
    hps = HEADS_PER_STEP
    qi = pl.program_id(2)
    feat = lax.broadcasted_iota(jnp.int32, (V_DIM, TQ), 0)
    hsl = [slice(hh * V_DIM, (hh + 1) * V_DIM) for hh in range(hps)]
    qst = []
    for hh in range(hps):
        qt = qt_ref[hsl[hh], :]
        zero = jnp.zeros_like(qt)
        qst.append(jnp.concatenate([jnp.where(feat < HEAD_DIM, qt, zero), jnp.where(feat >= HEAD_DIM, qt, zero)],
                                   axis=1))
    m_ref[...] = jnp.full(m_ref.shape, NEG, F32)
    acc_ref[...] = jnp.zeros(acc_ref.shape, F32)
    ones = jnp.ones((ONES_ROWS, TQ), BF16)

    def scores(kv, slot):
        start = pl.multiple_of(kv * TQ, TQ)
        for hh in range(hps):
            s_ref[slot, hh] = jnp.dot(k_ref[pl.ds(start, TQ), hsl[hh]], qst[hh], preferred_element_type=F32)

    def softmax_pv(kv, slot, masked):
        start = pl.multiple_of(kv * TQ, TQ)
        ps, corrs = [], []
        for hh in range(hps):
            s = s_ref[slot, hh]
            if masked:
                key = lax.broadcasted_iota(jnp.int32, s.shape, 0)
                qry = lax.broadcasted_iota(jnp.int32, s.shape, 1) % TQ
                s = jnp.where(key <= qry, s, NEG)
            m_old = m_ref[hh]
            m_new = jnp.maximum(m_old, jnp.max(s, axis=0, keepdims=True))
            corrs.append(jnp.exp2(m_old - m_new))
            ps.append(jnp.exp2(s - m_new).astype(BF16))
            m_ref[hh] = m_new
        for hh in range(hps):
            vt1 = jnp.concatenate([vt_ref[hsl[hh], pl.ds(start, TQ)], ones], axis=0)
            acc_ref[hh] = acc_ref[hh] * corrs[hh] + jnp.dot(vt1, ps[hh], preferred_element_type=F32)

    scores(0, 0)

    def pair(j, carry):
        kv = 2 * j
        scores(kv + 1, 1)
        softmax_pv(kv, 0, False)
        scores(kv + 2, 0)
        softmax_pv(kv + 1, 1, False)
        return carry

    lax.fori_loop(0, qi // 2, pair, 0)

    @pl.when(qi % 2 == 1)
    def _odd():
        scores(qi, 1)
        softmax_pv(qi - 1, 0, False)
        softmax_pv(qi, 1, True)

    @pl.when(qi % 2 == 0)
    def _even():
        softmax_pv(qi, 0, True)

    lam = _diff_lambda(lq1, lk1, lq2, lk2, lam0)
    for hh in range(hps):
        acc = acc_ref[hh]
        nrm = acc[:V_DIM] / acc[V_DIM:V_DIM + 1]
        ot = nrm[:, :TQ] - lam * nrm[:, TQ:]
        o_ref[:, hsl[hh]] = _subln(ot.T, gsub_ref[...], lam0).astype(o_ref.dtype)


def _prompt_attn(qt, k, vt, lams, g_sub, lam0):
    b, s, _ = k.shape
    hps = HEADS_PER_STEP
    w = hps * V_DIM
    qblk = pl.BlockSpec((None, w, TQ), lambda bi, h, qi: (bi, h, qi))
    vblk = pl.BlockSpec((None, w, s), lambda bi, h, qi: (bi, h, 0))
    kblk = pl.BlockSpec((None, s, w), lambda bi, h, qi: (bi, 0, h))
    oblk = pl.BlockSpec((None, TQ, w), lambda bi, h, qi: (bi, qi, h))
    small = lambda shape: pl.BlockSpec(shape, lambda bi, h, qi: (0, 0))
    return pl.pallas_call(
        functools.partial(_prompt_attn_kernel, lam0=lam0),
        grid=(b, N_HEADS // hps, s // TQ),
        in_specs=[qblk, kblk, vblk] + [small((1, HEAD_DIM))] * 4 + [small((1, V_DIM))],
        out_specs=oblk,
        out_shape=jax.ShapeDtypeStruct((b, s, ATT_W), BF16),
        scratch_shapes=[pltpu.VMEM((hps, 1, 2 * TQ), F32), pltpu.VMEM((hps, V_DIM + ONES_ROWS, 2 * TQ), F32),
                        pltpu.VMEM((2, hps, TQ, 2 * TQ), F32)],
        compiler_params=pltpu.CompilerParams(dimension_semantics=("parallel", "parallel", "arbitrary"),
                                             vmem_limit_bytes=VMEM_LIMIT),
        name="prompt_attn",
    )(qt, k, vt, *lams, g_sub)


def _sample_attn_kernel(pt_ref, q_ref, kn_ref, vn_ref, lq1, lk1, lq2, lk2, gsub_ref, *rest, lam0):
    del pt_ref
    pp = PAGES_PER_STEP
    k_refs, v_refs = rest[:pp], rest[pp:2 * pp]
    o_ref, qh_ref, m_ref, l_ref, acc_ref = rest[2 * pp:]
    g = pl.program_id(1)
    hr = 2 * SUBLANES
    heads = [slice(h * hr, (h + 1) * hr) for h in range(N_HEADS)]
    lanes = [slice(h * V_DIM, (h + 1) * V_DIM) for h in range(N_HEADS)]

    @pl.when(g == 0)
    def _init():
        q = q_ref[...]
        lane = lax.broadcasted_iota(jnp.int32, (SUBLANES, V_DIM), 1)
        for h in range(N_HEADS):
            qh = q[:, lanes[h]]
            qh_ref[heads[h], :] = jnp.concatenate(
                [jnp.where(lane < HEAD_DIM, qh, 0.0), jnp.where(lane >= HEAD_DIM, qh, 0.0)], axis=0).astype(BF16)
        m_ref[...] = jnp.full(m_ref.shape, NEG, F32)
        l_ref[...] = jnp.zeros(l_ref.shape, F32)
        acc_ref[...] = jnp.zeros(acc_ref.shape, F32)

    def update(s, pv):
        m_old = m_ref[...]
        m_new = jnp.maximum(m_old, jnp.max(s, axis=1, keepdims=True))
        corr = jnp.exp(m_old - m_new)
        p = jnp.exp(s - m_new)
        l_ref[...] = l_ref[...] * corr + jnp.sum(p, axis=1, keepdims=True)
        p = p.astype(BF16)
        acc_ref[...] = acc_ref[...] * corr + jnp.concatenate([pv(h, p[heads[h], :]) for h in range(N_HEADS)], axis=0)
        m_ref[...] = m_new

    def page_scores(h):
        kt = jnp.concatenate([r[lanes[h], :].astype(BF16) for r in k_refs], axis=1)
        return jnp.dot(qh_ref[heads[h], :], kt, preferred_element_type=F32)

    def page_pv(h, p):
        vh = jnp.concatenate([r[pl.ds(h, PAGE_SIZE, stride=N_HEADS), :].astype(BF16) for r in v_refs], axis=0)
        return jnp.dot(p, vh, preferred_element_type=F32)

    update(jnp.concatenate([page_scores(h) for h in range(N_HEADS)], axis=0), page_pv)

    @pl.when(g == pl.num_programs(1) - 1)
    def _finish():
        s = jnp.concatenate([_nt_dot(qh_ref[heads[h], :], kn_ref[:, lanes[h]]) for h in range(N_HEADS)], axis=0)
        t = lax.broadcasted_iota(jnp.int32, s.shape, 0) % SUBLANES
        j = lax.broadcasted_iota(jnp.int32, s.shape, 1)
        s = jnp.where(j <= t, s, NEG)
        update(s, lambda h, p: jnp.dot(p, vn_ref[:, lanes[h]], preferred_element_type=F32))
        lam = _diff_lambda(lq1, lk1, lq2, lk2, lam0)
        nrm = acc_ref[...] / l_ref[...]
        for h in range(N_HEADS):
            blk = nrm[heads[h], :]
            o = blk[:SUBLANES] - lam * blk[SUBLANES:]
            o_ref[:, lanes[h]] = _subln(o, gsub_ref[...], lam0)


def _sample_attn(q, k_new, v_new, cache_kt, cache_v2, page_table, lams, g_sub, lam0):
    db, n_pages = page_table.shape
    pp = PAGES_PER_STEP
    t = q.shape[0] // db
    rows = N_HEADS * 2 * SUBLANES
    kpage = lambda r: pl.BlockSpec((None, ATT_W, PAGE_SIZE), lambda b, g, pt: (pt[b, g * pp + r], 0, 0))
    vpage = lambda r: pl.BlockSpec((None, PAGE_SIZE * N_HEADS, V_DIM), lambda b, g, pt: (pt[b, g * pp + r], 0, 0))
    small = lambda shape: pl.BlockSpec(shape, lambda b, g, pt: (0, 0))
    new = pl.BlockSpec((None, NEW_PAD, ATT_W), lambda b, g, pt: (b, 0, 0))
    qblk = pl.BlockSpec((t, ATT_W), lambda b, g, pt: (b, 0))
    grid_spec = pltpu.PrefetchScalarGridSpec(
        num_scalar_prefetch=1,
        grid=(db, n_pages // pp),
        in_specs=[qblk, new, new] + [small((1, HEAD_DIM))] * 4 + [small((1, V_DIM))]
                 + [kpage(r) for r in range(pp)] + [vpage(r) for r in range(pp)],
        out_specs=qblk,
        scratch_shapes=[pltpu.VMEM((rows, V_DIM), BF16), pltpu.VMEM((rows, 1), F32), pltpu.VMEM((rows, 1), F32),
                        pltpu.VMEM((rows, V_DIM), F32)],
    )
    return pl.pallas_call(
        functools.partial(_sample_attn_kernel, lam0=lam0),
        grid_spec=grid_spec,
        out_shape=jax.ShapeDtypeStruct(q.shape, F32),
        compiler_params=pltpu.CompilerParams(dimension_semantics=("parallel", "arbitrary"),
                                             vmem_limit_bytes=VMEM_LIMIT),
        name="sample_attn",
    )(page_table, q, k_new, v_new, *lams, g_sub, *([cache_kt] * pp), *([cache_v2] * pp))


def _mix_weights(sgu_w, sgu_b, chunk_len):
    wm = (sgu_w * jnp.tril(jnp.ones((CHUNK, CHUNK), sgu_w.dtype)))[:, :chunk_len, :chunk_len]
    eye = jnp.eye(TM // chunk_len, dtype=sgu_w.dtype)
    mixw = jnp.einsum("ab,gij->gaibj", eye, wm).reshape(SGU_GROUPS, TM, TM).astype(BF16)
    bias = jnp.repeat(sgu_b[:, :chunk_len].T, D_SGU // SGU_GROUPS, axis=1)
    return mixw, jnp.tile(bias, (TM // chunk_len, 1))


def _post_kernel(x_ref, o_ref, u_ref, vn_ref, ga_ref, gb_ref, mixw_ref, mixb_ref, wo_ref, gmix_ref,
                 prev_ref, gpre_ref, wup_ref, cw_ref, cb_ref, wdn_ref, gpost_ref,
                 y_ref, a_ref, carry_ref, *, tiles_per_row, sample):
    vn = vn_ref[...].astype(BF16)
    parts = []
    for g in range(SGU_GROUPS):
        z = jnp.dot(mixw_ref[g], vn[:, g * LANES:(g + 1) * LANES], preferred_element_type=F32)
        parts.append(z + mixb_ref[:, g * LANES:(g + 1) * LANES])
    s_out = u_ref[...].astype(F32) * jnp.concatenate(parts, axis=1)
    merged = ga_ref[...].astype(F32) * o_ref[...].astype(F32) + gb_ref[...].astype(F32) * s_out
    h = x_ref[...] + _rms(jnp.dot(merged.astype(BF16), wo_ref[...], preferred_element_type=F32), gmix_ref[...])

    hn = _rms(h, gpre_ref[...]).astype(BF16)
    if not sample:
        @pl.when(pl.program_id(0) % tiles_per_row == 0)
        def _():
            carry_ref[...] = jnp.zeros(carry_ref.shape, F32)

    out = None
    for lo, hi in FF_CHUNKS:
        a = jnp.dot(hn, wup_ref[:, lo:hi], preferred_element_type=F32)
        b = jnp.dot(hn, wup_ref[:, D_FF + lo:D_FF + hi], preferred_element_type=F32)
        am1 = pltpu.roll(a, 1, 0)
        am2 = pltpu.roll(a, 2, 0)
        if sample:
            prev = prev_ref[:, lo:hi]
            t = lax.broadcasted_iota(jnp.int32, a.shape, 0) % SUBLANES
            am1 = jnp.where(t == 0, pltpu.roll(prev, TM - 1, 0), am1)
            am2 = jnp.where(t < 2, prev, am2)
            a_ref[:, lo:hi] = a
        else:
            last = carry_ref[:, lo:hi]
            t = lax.broadcasted_iota(jnp.int32, last.shape, 0)
            head1 = jnp.where(t == 0, pltpu.roll(last, 1, 0), am1[:SUBLANES])
            head2 = jnp.where(t < 2, pltpu.roll(last, 2, 0), am2[:SUBLANES])
            am1 = jnp.concatenate([head1, am1[SUBLANES:]], axis=0)
            am2 = jnp.concatenate([head2, am2[SUBLANES:]], axis=0)
            carry_ref[:, lo:hi] = a[TM - SUBLANES:]
            a_ref[:, lo:hi] = a[TM - SUBLANES:]
        c = cb_ref[:, lo:hi] + cw_ref[0:1, lo:hi] * am2 + cw_ref[1:2, lo:hi] * am1 + cw_ref[2:3, lo:hi] * a
        gated = (jax.nn.gelu(c) * b).astype(BF16)
        part = jnp.dot(gated, wdn_ref[lo:hi, :], preferred_element_type=F32)
        out = part if out is None else out + part
    y_ref[...] = h + _rms(out, gpost_ref[...])


def _post(x, o, u, vn, ga, gb, mixw, mixb, w_o, g_mix_post, prev, g_pre, w_up, conv_w, conv_b, w_down, g_post,
          *, tiles_per_row, sample):
    n = x.shape[0]
    row = lambda i: (i, 0)
    act = pl.BlockSpec((TM, D_MODEL), row)
    if sample:
        prev_spec = pl.BlockSpec((TM, D_FF), row)
        a_spec = pl.BlockSpec((TM, D_FF), row)
        a_shape = jax.ShapeDtypeStruct((n, D_FF), F32)
    else:
        prev_spec = _full(prev.shape)
        a_spec = pl.BlockSpec((None, SUBLANES, D_FF), lambda i: (i, 0, 0))
        a_shape = jax.ShapeDtypeStruct((n // TM, SUBLANES, D_FF), F32)
    return pl.pallas_call(
        functools.partial(_post_kernel, tiles_per_row=tiles_per_row, sample=sample),
        grid=(n // TM,),
        in_specs=[act] * 6 + [_full((SGU_GROUPS, TM, TM)), _full((TM, D_SGU)), _full((D_MODEL, D_MODEL)),
                              _full((1, D_MODEL)), prev_spec, _full((1, D_MODEL)), _full((D_MODEL, 2 * D_FF)),
                              _full((CONV_W, D_FF)), _full((1, D_FF)), _full((D_FF, D_MODEL)), _full((1, D_MODEL))],
        out_specs=[act, a_spec],
        out_shape=[jax.ShapeDtypeStruct((n, D_MODEL), F32), a_shape],
        scratch_shapes=[pltpu.VMEM((SUBLANES, D_FF), F32)],
        compiler_params=pltpu.CompilerParams(dimension_semantics=("arbitrary",), vmem_limit_bytes=VMEM_LIMIT),
        name="post",
    )(x, o, u, vn, ga, gb, mixw, mixb, w_o, g_mix_post, prev, g_pre, w_up, conv_w, conv_b, w_down, g_post)


def kernel(x_prompt, x_sample, cache_k, cache_v, state_conv, page_table, w_in, w_o, g_mix_pre, g_mix_post,
           lambda_q1, lambda_k1, lambda_q2, lambda_k2, g_subln, sgu_ln_g, sgu_ln_b, sgu_w, sgu_b, g_ffn_pre,
           g_ffn_post, w_up, conv_w, conv_b, w_down):
    depth = w_in.shape[0]
    assert depth == 1, "one layer per step"
    b, s, _ = x_prompt.shape
    db, t, _ = x_sample.shape
    n_pool, n_pages = cache_k.shape[1], page_table.shape[1]
    past_len = n_pages * PAGE_SIZE
    assert s % TM == 0 and s % TQ == 0 and (db * t) % TM == 0 and t == SUBLANES and TM % CHUNK == 0
    assert n_pages % PAGES_PER_STEP == 0 and N_HEADS % HEADS_PER_STEP == 0

    lam0 = 0.8 - 0.6 * math.exp(-0.3 * 0)
    lams = (lambda_q1, lambda_k1, lambda_q2, lambda_k2)
    w_in_b, w_o_b = w_in[0].astype(BF16), w_o[0].astype(BF16)
    w_up_b, w_down_b = w_up[0].astype(BF16), w_down[0].astype(BF16)
    proj_w = (w_in_b, g_mix_pre, sgu_ln_g, sgu_ln_b)
    ffn_w = (g_ffn_pre, w_up_b, conv_w[0], conv_b, w_down_b, g_ffn_post)

    xp = x_prompt.reshape(b * s, D_MODEL)
    qpt, kpt, kpb, vpf, vpt, up, vnp, gap, gbp = _in_proj(
        xp, *proj_w, _rope_tables(jnp.arange(s)), q_dtype=BF16, vn_dtype=BF16, seq_len=s)
    op = _prompt_attn(qpt, kpb.reshape(b, s, ATT_W), vpt, lams, g_subln, lam0).reshape(b * s, ATT_W)
    k_prompt = kpt.reshape(1, b, N_HEADS, 2, HEAD_DIM, s).transpose(0, 1, 5, 2, 3, 4)
    mix_p = _mix_weights(sgu_w[0], sgu_b[0], CHUNK)
    yp, ap = _post(xp, op, up, vnp, gap, gbp, *mix_p, w_o_b, g_mix_post, jnp.zeros((SUBLANES, LANES), F32), *ffn_w,
                   tiles_per_row=s // TM, sample=False)
    conv_p = ap.reshape(b, s // TM, SUBLANES, D_FF)[:, -1, SUBLANES - (CONV_W - 1):]

    xs = x_sample.reshape(db * t, D_MODEL)
    pos_s = jnp.tile(past_len + jnp.arange(t), TM // t)
    qs, ksf, ksb, vsf, vsb, us, vns, gas, gbs = _in_proj(
        xs, *proj_w, _rope_tables(pos_s), q_dtype=F32, vn_dtype=F32)
    pad_new = lambda a: jnp.pad(a.reshape(db, t, ATT_W), ((0, 0), (0, NEW_PAD - t), (0, 0)))
    ckt = cache_k[0].transpose(0, 2, 3, 4, 1).reshape(n_pool, ATT_W, PAGE_SIZE)
    cv2 = cache_v[0].reshape(n_pool, PAGE_SIZE * N_HEADS, V_DIM)
    os_ = _sample_attn(qs, pad_new(ksb), pad_new(vsb), ckt, cv2, page_table, lams, g_subln, lam0)
    mix_s = _mix_weights(sgu_w[0], sgu_b[0], t)
    prev = jnp.pad(state_conv[0], ((0, 0), (0, t - (CONV_W - 1)), (0, 0))).reshape(db * t, D_FF)
    ys, as_ = _post(xs, os_, us, vns, gas, gbs, *mix_s, w_o_b, g_mix_post, prev, *ffn_w,
                    tiles_per_row=1, sample=True)
    conv_s = as_.reshape(db, t, D_FF)[:, t - (CONV_W - 1):]

    return (yp.reshape(b, s, D_MODEL), ys.reshape(db, t, D_MODEL),
            k_prompt, vpf.reshape(1, b, s, N_HEADS, V_DIM),
            conv_p[None], ksf.reshape(1, db, t, N_HEADS, 2, HEAD_DIM), vsf.reshape(1, db, t, N_HEADS, V_DIM),
            vns.reshape(1, db, t, D_SGU), conv_s[None])
```

```python
import functools
import math

import jax
import jax.numpy as jnp
from jax import lax
from jax.experimental import pallas as pl
from jax.experimental.pallas import tpu as pltpu

F32 = jnp.float32
BF16 = jnp.bfloat16

D_MODEL = 1024
N_HEADS = 8
HEAD_DIM = 64
V_DIM = 2 * HEAD_DIM
ATT_W = N_HEADS * 2 * HEAD_DIM
ROT_DIM = HEAD_DIM // 4
ROPE_THETA = 500000.0
SGU_GROUPS = 8
CHUNK = 128
D_SGU = 1024
D_FF = 2816
CONV_W = 3
EPS = 1e-6
PAGE_SIZE = 128
D_IN = 3 * ATT_W + 2 * D_SGU + 2 * D_MODEL
NEG = -1e30

LANES = 128
SUBLANES = 8
VMEM_LIMIT = 56 * 1024 * 1024

TM = 256
TQ = 256
HEADS_PER_STEP = 4
PAGES_PER_STEP = 16
NEW_PAD = 16
ONES_ROWS = 16
FF_CHUNKS = ((0, 1536), (1536, D_FF))
LOG2E = 1.4426950408889634


def _rms(x, g):
    return x * lax.rsqrt(jnp.mean(x * x, axis=-1, keepdims=True) + EPS) * g


def _nt_dot(a, b):
    return lax.dot_general(a, b, (((1,), (1,)), ((), ())), preferred_element_type=F32)


def _full(shape):
    return pl.BlockSpec(shape, lambda i: (0,) * len(shape), pipeline_mode=pl.Buffered(1))


def _in_proj_kernel(x_ref, w_ref, gpre_ref, lng_ref, lnb_ref, cos_ref, sa_ref, sb_ref,
                    q_ref, kf_ref, kb_ref, vf_ref, vb_ref, u_ref, vn_ref, ga_ref, gb_ref, *, transposed):
    xn = _rms(x_ref[...], gpre_ref[...]).astype(BF16)
    q_scale = HEAD_DIM ** -0.5 * (LOG2E if transposed else 1.0)
    heads = [slice(h * V_DIM, (h + 1) * V_DIM) for h in range(N_HEADS)]

    def proj(seg):
        return jnp.dot(xn, w_ref[:, seg * ATT_W:(seg + 1) * ATT_W], preferred_element_type=F32)

    cos, sa, sb = cos_ref[...], sa_ref[...], sb_ref[...]

    def rope(p, h):
        s = p[:, heads[h]]
        return s * cos + pltpu.roll(s, LANES - ROT_DIM // 2, 1) * sa + pltpu.roll(s, ROT_DIM // 2, 1) * sb

    p = proj(0)
    for h in range(N_HEADS):
        r = rope(p, h) * q_scale
        if transposed:
            q_ref[heads[h], :] = r.T.astype(q_ref.dtype)
        else:
            q_ref[:, heads[h]] = r.astype(q_ref.dtype)
    p = proj(1)
    for h in range(N_HEADS):
        r = rope(p, h)
        if transposed:
            kf_ref[heads[h], :] = r.T
        else:
            kf_ref[:, heads[h]] = r
        kb_ref[:, heads[h]] = r.astype(BF16)
    p = proj(2)
    vf_ref[...] = p
    if transposed:
        for h in range(N_HEADS):
            vb_ref[heads[h], :] = p[:, heads[h]].T.astype(BF16)
    else:
        vb_ref[...] = p.astype(BF16)
    u_ref[...] = jax.nn.gelu(proj(3)).astype(BF16)
    g = jax.nn.gelu(proj(4))
    mu = jnp.mean(g, axis=-1, keepdims=True)
    gc = g - mu
    vn = gc * lax.rsqrt(jnp.mean(gc * gc, axis=-1, keepdims=True) + EPS) * lng_ref[...] + lnb_ref[...]
    vn_ref[...] = vn.astype(vn_ref.dtype)
    ga_ref[...] = jax.nn.sigmoid(proj(5)).astype(BF16)
    gb_ref[...] = jax.nn.sigmoid(proj(6)).astype(BF16)


def _in_proj(x, w_in, g_pre, ln_g, ln_b, tabs, *, q_dtype, vn_dtype, seq_len=None):
    n = x.shape[0]
    n_tab = tabs[0].shape[0] // TM
    row = lambda i: (i, 0)
    tab = lambda i: (i % n_tab, 0)
    act = pl.BlockSpec((TM, D_MODEL), row)
    outs = [(q_dtype, ATT_W), (F32, ATT_W), (BF16, ATT_W), (F32, ATT_W), (BF16, ATT_W),
            (BF16, D_SGU), (vn_dtype, D_SGU), (BF16, D_MODEL), (BF16, D_MODEL)]
    out_specs = [pl.BlockSpec((TM, w), row) for _, w in outs]
    out_shape = [jax.ShapeDtypeStruct((n, w), d) for d, w in outs]
    if seq_len is not None:
        tpr = seq_len // TM
        for idx in (0, 1, 4):
            out_specs[idx] = pl.BlockSpec((None, ATT_W, TM), lambda i: (i // tpr, 0, i % tpr))
            out_shape[idx] = jax.ShapeDtypeStruct((n // seq_len, ATT_W, seq_len), outs[idx][0])
    return pl.pallas_call(
        functools.partial(_in_proj_kernel, transposed=seq_len is not None),
        grid=(n // TM,),
        in_specs=[act, _full((D_MODEL, D_IN)), _full((1, D_MODEL)), _full((1, D_SGU)), _full((1, D_SGU)),
                  pl.BlockSpec((TM, LANES), tab), pl.BlockSpec((TM, LANES), tab), pl.BlockSpec((TM, LANES), tab)],
        out_specs=out_specs,
        out_shape=out_shape,
        compiler_params=pltpu.CompilerParams(dimension_semantics=("parallel",), vmem_limit_bytes=VMEM_LIMIT),
        name="in_proj",
    )(x, w_in, g_pre, ln_g, ln_b, *tabs)


def _rope_tables(pos):
    half = ROT_DIM // 2
    inv = ROPE_THETA ** (-jnp.arange(0, ROT_DIM, 2, dtype=F32) / ROT_DIM)
    ang = pos.astype(F32)[:, None] * inv[None, :]
    cos, sin = jnp.cos(ang), jnp.sin(ang)
    d = jnp.arange(LANES) % HEAD_DIM
    lo, hi = d < half, (d >= half) & (d < ROT_DIM)
    idx = d % half
    c = jnp.where((lo | hi)[None, :], cos[:, idx], 1.0)
    sa = jnp.where(lo[None, :], -sin[:, idx], 0.0)
    sb = jnp.where(hi[None, :], sin[:, idx], 0.0)
    return c, sa, sb


def _diff_lambda(lq1, lk1, lq2, lk2, lam0):
    s1 = jnp.sum(lq1[...] * lk1[...], axis=-1, keepdims=True)
    s2 = jnp.sum(lq2[...] * lk2[...], axis=-1, keepdims=True)
    return jnp.exp(s1) - jnp.exp(s2) + lam0


def _subln(o, g, lam0):
    return _rms(o, g) * (1.0 - lam0)


def _prompt_attn_kernel(qt_ref, k_ref, vt_ref, lq1, lk1, lq2, lk2, gsub_ref, o_ref, m_ref, acc_ref, s_ref, *, lam0):
    hps = HEADS_PER_STEP
    qi = pl.program_id(2)
    feat = lax.broadcasted_iota(jnp.int32, (V_DIM, TQ), 0)
    hsl = [slice(hh * V_DIM, (hh + 1) * V_DIM) for hh in range(hps)]
    qst = []
    for hh in range(hps):
        qt = qt_ref[hsl[hh], :]
        zero = jnp.zeros_like(qt)
        qst.append(jnp.concatenate([jnp.where(feat < HEAD_DIM, qt, zero), jnp.where(feat >= HEAD_DIM, qt, zero)],
                                   axis=1))
    m_ref[...] = jnp.full(m_ref.shape, NEG, F32)
    acc_ref[...] = jnp.zeros(acc_ref.shape, F32)
    ones = jnp.ones((ONES_ROWS, TQ), BF16)

    def scores(kv, slot):
        start = pl.multiple_of(kv * TQ, TQ)
        for hh in range(hps):
            s_ref[slot, hh] = jnp.dot(k_ref[pl.ds(start, TQ), hsl[hh]], qst[hh], preferred_element_type=F32)

    def softmax_pv(kv, slot, masked):
        start = pl.multiple_of(kv * TQ, TQ)
        ps, corrs = [], []
        for hh in range(hps):
            s = s_ref[slot, hh]
            if masked:
                key = lax.broadcasted_iota(jnp.int32, s.shape, 0)
                qry = lax.broadcasted_iota(jnp.int32, s.shape, 1) % TQ
                s = jnp.where(key <= qry, s, NEG)
            m_old = m_ref[hh]
            m_new = jnp.maximum(m_old, jnp.max(s, axis=0, keepdims=True))
            corrs.append(jnp.exp2(m_old - m_new))
            ps.append(jnp.exp2(s - m_new).astype(BF16))
            m_ref[hh] = m_new
        for hh in range(hps):
            vt1 = jnp.concatenate([vt_ref[hsl[hh], pl.ds(start, TQ)], ones], axis=0)
            acc_ref[hh] = acc_ref[hh] * corrs[hh] + jnp.dot(vt1, ps[hh], preferred_element_type=F32)

    scores(0, 0)

    def pair(j, carry):
        kv = 2 * j
        scores(kv + 1, 1)
        softmax_pv(kv, 0, False)
        scores(kv + 2, 0)
        softmax_pv(kv + 1, 1, False)
        return carry

    lax.fori_loop(0, qi // 2, pair, 0)

    @pl.when(qi % 2 == 1)
    def _odd():
        scores(qi, 1)
        softmax_pv(qi - 1, 0, False)
        softmax_pv(qi, 1, True)

    @pl.when(qi % 2 == 0)
    def _even():
        softmax_pv(qi, 0, True)

    lam = _diff_lambda(lq1, lk1, lq2, lk2, lam0)
    for hh in range(hps):
        acc = acc_ref[hh]
        nrm = acc[:V_DIM] / acc[V_DIM:V_DIM + 1]
        ot = nrm[:, :TQ] - lam * nrm[:, TQ:]
        o_ref[:, hsl[hh]] = _subln(ot.T, gsub_ref[...], lam0).astype(o_ref.dtype)


def _prompt_attn(qt, k, vt, lams, g_sub, lam0):
    b, s, _ = k.shape
    hps = HEADS_PER_STEP
    w = hps * V_DIM
    qblk = pl.BlockSpec((None, w, TQ), lambda bi, h, qi: (bi, h, qi))
    vblk = pl.BlockSpec((None, w, s), lambda bi, h, qi: (bi, h, 0))
    kblk = pl.BlockSpec((None, s, w), lambda bi, h, qi: (bi, 0, h))
    oblk = pl.BlockSpec((None, TQ, w), lambda bi, h, qi: (bi, qi, h))
    small = lambda shape: pl.BlockSpec(shape, lambda bi, h, qi: (0, 0))
    return pl.pallas_call(
        functools.partial(_prompt_attn_kernel, lam0=lam0),
        grid=(b, N_HEADS // hps, s // TQ),
        in_specs=[qblk, kblk, vblk] + [small((1, HEAD_DIM))] * 4 + [small((1, V_DIM))],
        out_specs=oblk,
        out_shape=jax.ShapeDtypeStruct((b, s, ATT_W), BF16),
        scratch_shapes=[pltpu.VMEM((hps, 1, 2 * TQ), F32), pltpu.VMEM((hps, V_DIM + ONES_ROWS, 2 * TQ), F32),
                        pltpu.VMEM((2, hps, TQ, 2 * TQ), F32)],
        compiler_params=pltpu.CompilerParams(dimension_semantics=("parallel", "parallel", "arbitrary"),
                                             vmem_limit_bytes=VMEM_LIMIT),
        name="prompt_attn",
    )(qt, k, vt, *lams, g_sub)


def _sample_attn_kernel(pt_ref, q_ref, kn_ref, vn_ref, lq1, lk1, lq2, lk2, gsub_ref, *rest, lam0):
    del pt_ref
    pp = PAGES_PER_STEP
    k_refs, v_refs = rest[:pp], rest[pp:2 * pp]
    o_ref, qh_ref, m_ref, l_ref, acc_ref = rest[2 * pp:]
    g = pl.program_id(1)
    hr = 2 * SUBLANES
    heads = [slice(h * hr, (h + 1) * hr) for h in range(N_HEADS)]
    lanes = [slice(h * V_DIM, (h + 1) * V_DIM) for h in range(N_HEADS)]

    @pl.when(g == 0)
    def _init():
        q = q_ref[...]
        lane = lax.broadcasted_iota(jnp.int32, (SUBLANES, V_DIM), 1)
        for h in range(N_HEADS):
            qh = q[:, lanes[h]]
            qh_ref[heads[h], :] = jnp.concatenate(
                [jnp.where(lane < HEAD_DIM, qh, 0.0), jnp.where(lane >= HEAD_DIM, qh, 0.0)], axis=0).astype(BF16)
        m_ref[...] = jnp.full(m_ref.shape, NEG, F32)
        l_ref[...] = jnp.zeros(l_ref.shape, F32)
        acc_ref[...] = jnp.zeros(acc_ref.shape, F32)

    def update(s, pv_all):
        m_old = m_ref[...]
        m_new = jnp.maximum(m_old, jnp.max(s, axis=1, keepdims=True))
        corr = jnp.exp(m_old - m_new)
        p = jnp.exp(s - m_new)
        l_ref[...] = l_ref[...] * corr + jnp.sum(p, axis=1, keepdims=True)
        m_ref[...] = m_new
        acc_ref[...] = acc_ref[...] * corr + pv_all(p.astype(BF16))

    def page_scores(h):
        kt = jnp.concatenate([r[lanes[h], :].astype(BF16) for r in k_refs], axis=1)
        return jnp.dot(qh_ref[heads[h], :], kt, preferred_element_type=F32)

    half = N_HEADS // 2
    tok = lax.broadcasted_iota(jnp.int32, (PAGE_SIZE, 2 * PAGE_SIZE), 0)
    col = lax.broadcasted_iota(jnp.int32, (PAGE_SIZE, 2 * PAGE_SIZE), 1)
    spread = [jnp.where(col == 2 * tok + j, 1.0, 0.0).astype(BF16) for j in range(2)]

    def page_pv_all(p):
        pex = []
        for j in range(2):
            rows = slice(j * half * hr, (j + 1) * half * hr)
            stack = jnp.concatenate([p[rows, pg * PAGE_SIZE:(pg + 1) * PAGE_SIZE] for pg in range(pp)], axis=0)
            pex.append(jnp.dot(stack, spread[j], preferred_element_type=F32).astype(BF16))
        outs = [None] * N_HEADS
        for h in range(half):
            lhs = jnp.concatenate(
                [jnp.concatenate([pex[j][pg * half * hr + h * hr: pg * half * hr + (h + 1) * hr, :] for j in range(2)],
                                 axis=0) for pg in range(pp)], axis=1)
            vp = jnp.concatenate([r[pl.ds(h, 2 * PAGE_SIZE, stride=half), :].astype(BF16) for r in v_refs], axis=0)
            o = jnp.dot(lhs, vp, preferred_element_type=F32)
            outs[h], outs[h + half] = o[:hr], o[hr:]
        return jnp.concatenate(outs, axis=0)

    update(jnp.concatenate([page_scores(h) for h in range(N_HEADS)], axis=0), page_pv_all)

    @pl.when(g == pl.num_programs(1) - 1)
    def _finish():
        s = jnp.concatenate([_nt_dot(qh_ref[heads[h], :], kn_ref[:, lanes[h]]) for h in range(N_HEADS)], axis=0)
        t = lax.broadcasted_iota(jnp.int32, s.shape, 0) % SUBLANES
        j = lax.broadcasted_iota(jnp.int32, s.shape, 1)
        s = jnp.where(j <= t, s, NEG)
        update(s, lambda p: jnp.concatenate(
            [jnp.dot(p[heads[h], :], vn_ref[:, lanes[h]], preferred_element_type=F32) for h in range(N_HEADS)], axis=0))
        lam = _diff_lambda(lq1, lk1, lq2, lk2, lam0)
        nrm = acc_ref[...] / l_ref[...]
        for h in range(N_HEADS):
            blk = nrm[heads[h], :]
            o = blk[:SUBLANES] - lam * blk[SUBLANES:]
            o_ref[:, lanes[h]] = _subln(o, gsub_ref[...], lam0)


def _sample_attn(q, k_new, v_new, cache_kt, cache_v2, page_table, lams, g_sub, lam0):
    db, n_pages = page_table.shape
    pp = PAGES_PER_STEP
    t = q.shape[0] // db
    rows = N_HEADS * 2 * SUBLANES
    kpage = lambda r: pl.BlockSpec((None, ATT_W, PAGE_SIZE), lambda b, g, pt: (pt[b, g * pp + r], 0, 0))
    vpage = lambda r: pl.BlockSpec((None, PAGE_SIZE * N_HEADS, V_DIM), lambda b, g, pt: (pt[b, g * pp + r], 0, 0))
    small = lambda shape: pl.BlockSpec(shape, lambda b, g, pt: (0, 0))
    new = pl.BlockSpec((None, NEW_PAD, ATT_W), lambda b, g, pt: (b, 0, 0))
    qblk = pl.BlockSpec((t, ATT_W), lambda b, g, pt: (b, 0))
    grid_spec = pltpu.PrefetchScalarGridSpec(
        num_scalar_prefetch=1,
        grid=(db, n_pages // pp),
        in_specs=[qblk, new, new] + [small((1, HEAD_DIM))] * 4 + [small((1, V_DIM))]
                 + [kpage(r) for r in range(pp)] + [vpage(r) for r in range(pp)],
        out_specs=qblk,
        scratch_shapes=[pltpu.VMEM((rows, V_DIM), BF16), pltpu.VMEM((rows, 1), F32), pltpu.VMEM((rows, 1), F32),
                        pltpu.VMEM((rows, V_DIM), F32)],
    )
    return pl.pallas_call(
        functools.partial(_sample_attn_kernel, lam0=lam0),
        grid_spec=grid_spec,
        out_shape=jax.ShapeDtypeStruct(q.shape, F32),
        compiler_params=pltpu.CompilerParams(dimension_semantics=("parallel", "arbitrary"),
                                             vmem_limit_bytes=VMEM_LIMIT),
        name="sample_attn",
    )(page_table, q, k_new, v_new, *lams, g_sub, *([cache_kt] * pp), *([cache_v2] * pp))


def _mix_weights(sgu_w, sgu_b, chunk_len):
    wm = (sgu_w * jnp.tril(jnp.ones((CHUNK, CHUNK), sgu_w.dtype)))[:, :chunk_len, :chunk_len]
    n = TM // chunk_len
    tiled = jnp.concatenate([jnp.concatenate([wm] * n, axis=2)] * n, axis=1)
    blk = jnp.arange(TM) // chunk_len
    mixw = jnp.where((blk[:, None] == blk[None, :])[None], tiled, 0.0).astype(BF16)
    bias = jnp.repeat(sgu_b[:, :chunk_len].T, D_SGU // SGU_GROUPS, axis=1)
    return mixw, jnp.tile(bias, (TM // chunk_len, 1))


def _post_kernel(x_ref, o_ref, u_ref, vn_ref, ga_ref, gb_ref, mixw_ref, mixb_ref, wo_ref, gmix_ref,
                 prev_ref, gpre_ref, wup_ref, cw_ref, cb_ref, wdn_ref, gpost_ref,
                 y_ref, a_ref, carry_ref, *, tiles_per_row, sample):
    vn = vn_ref[...].astype(BF16)
    parts = []
    for g in range(SGU_GROUPS):
        z = jnp.dot(mixw_ref[g], vn[:, g * LANES:(g + 1) * LANES], preferred_element_type=F32)
        parts.append(z + mixb_ref[:, g * LANES:(g + 1) * LANES])
    s_out = u_ref[...].astype(F32) * jnp.concatenate(parts, axis=1)
    merged = ga_ref[...].astype(F32) * o_ref[...].astype(F32) + gb_ref[...].astype(F32) * s_out
    h = x_ref[...] + _rms(jnp.dot(merged.astype(BF16), wo_ref[...], preferred_element_type=F32), gmix_ref[...])

    hn = _rms(h, gpre_ref[...]).astype(BF16)
    if not sample:
        @pl.when(pl.program_id(0) % tiles_per_row == 0)
        def _():
            carry_ref[...] = jnp.zeros(carry_ref.shape, F32)

    out = None
    for lo, hi in FF_CHUNKS:
        a = jnp.dot(hn, wup_ref[:, lo:hi], preferred_element_type=F32)
        b = jnp.dot(hn, wup_ref[:, D_FF + lo:D_FF + hi], preferred_element_type=F32)
        am1 = pltpu.roll(a, 1, 0)
        am2 = pltpu.roll(a, 2, 0)
        if sample:
            prev = prev_ref[:, lo:hi]
            t = lax.broadcasted_iota(jnp.int32, a.shape, 0) % SUBLANES
            am1 = jnp.where(t == 0, pltpu.roll(prev, TM - 1, 0), am1)
            am2 = jnp.where(t < 2, prev, am2)
            a_ref[:, lo:hi] = a
        else:
            last = carry_ref[:, lo:hi]
            t = lax.broadcasted_iota(jnp.int32, last.shape, 0)
            head1 = jnp.where(t == 0, pltpu.roll(last, 1, 0), am1[:SUBLANES])
            head2 = jnp.where(t < 2, pltpu.roll(last, 2, 0), am2[:SUBLANES])
            am1 = jnp.concatenate([head1, am1[SUBLANES:]], axis=0)
            am2 = jnp.concatenate([head2, am2[SUBLANES:]], axis=0)
            carry_ref[:, lo:hi] = a[TM - SUBLANES:]
            a_ref[:, lo:hi] = a[TM - SUBLANES:]
        c = cb_ref[:, lo:hi] + cw_ref[0:1, lo:hi] * am2 + cw_ref[1:2, lo:hi] * am1 + cw_ref[2:3, lo:hi] * a
        gated = (jax.nn.gelu(c) * b).astype(BF16)
        part = jnp.dot(gated, wdn_ref[lo:hi, :], preferred_element_type=F32)
        out = part if out is None else out + part
    y_ref[...] = h + _rms(out, gpost_ref[...])


def _post(x, o, u, vn, ga, gb, mixw, mixb, w_o, g_mix_post, prev, g_pre, w_up, conv_w, conv_b, w_down, g_post,
          *, tiles_per_row, sample):
    n = x.shape[0]
    row = lambda i: (i, 0)
    act = pl.BlockSpec((TM, D_MODEL), row)
    if sample:
        prev_spec = pl.BlockSpec((TM, D_FF), row)
        a_spec = pl.BlockSpec((TM, D_FF), row)
        a_shape = jax.ShapeDtypeStruct((n, D_FF), F32)
    else:
        prev_spec = _full(prev.shape)
        a_spec = pl.BlockSpec((None, SUBLANES, D_FF), lambda i: (i, 0, 0))
        a_shape = jax.ShapeDtypeStruct((n // TM, SUBLANES, D_FF), F32)
    return pl.pallas_call(
        functools.partial(_post_kernel, tiles_per_row=tiles_per_row, sample=sample),
        grid=(n // TM,),
        in_specs=[act] * 6 + [_full((SGU_GROUPS, TM, TM)), _full((TM, D_SGU)), _full((D_MODEL, D_MODEL)),
                              _full((1, D_MODEL)), prev_spec, _full((1, D_MODEL)), _full((D_MODEL, 2 * D_FF)),
                              _full((CONV_W, D_FF)), _full((1, D_FF)), _full((D_FF, D_MODEL)), _full((1, D_MODEL))],
        out_specs=[act, a_spec],
        out_shape=[jax.ShapeDtypeStruct((n, D_MODEL), F32), a_shape],
        scratch_shapes=[pltpu.VMEM((SUBLANES, D_FF), F32)],
        compiler_params=pltpu.CompilerParams(dimension_semantics=("arbitrary",), vmem_limit_bytes=VMEM_LIMIT),
        name="post",
    )(x, o, u, vn, ga, gb, mixw, mixb, w_o, g_mix_post, prev, g_pre, w_up, conv_w, conv_b, w_down, g_post)


def kernel(x_prompt, x_sample, cache_k, cache_v, state_conv, page_table, w_in, w_o, g_mix_pre, g_mix_post,
           lambda_q1, lambda_k1, lambda_q2, lambda_k2, g_subln, sgu_ln_g, sgu_ln_b, sgu_w, sgu_b, g_ffn_pre,
           g_ffn_post, w_up, conv_w, conv_b, w_down):
    depth = w_in.shape[0]
    assert depth == 1, "one layer per step"
    b, s, _ = x_prompt.shape
    db, t, _ = x_sample.shape
    n_pool, n_pages = cache_k.shape[1], page_table.shape[1]
    past_len = n_pages * PAGE_SIZE
    assert s % TM == 0 and s % TQ == 0 and (db * t) % TM == 0 and t == SUBLANES and TM % CHUNK == 0
    assert n_pages % PAGES_PER_STEP == 0 and N_HEADS % HEADS_PER_STEP == 0

    lam0 = 0.8 - 0.6 * math.exp(-0.3 * 0)
    lams = (lambda_q1, lambda_k1, lambda_q2, lambda_k2)
    w_in_b, w_o_b = w_in[0].astype(BF16), w_o[0].astype(BF16)
    w_up_b, w_down_b = w_up[0].astype(BF16), w_down[0].astype(BF16)
    proj_w = (w_in_b, g_mix_pre, sgu_ln_g, sgu_ln_b)
    ffn_w = (g_ffn_pre, w_up_b, conv_w[0], conv_b, w_down_b, g_ffn_post)

    xp = x_prompt.reshape(b * s, D_MODEL)
    qpt, kpt, kpb, vpf, vpt, up, vnp, gap, gbp = _in_proj(
        xp, *proj_w, _rope_tables(jnp.arange(s)), q_dtype=BF16, vn_dtype=BF16, seq_len=s)
    op = _prompt_attn(qpt, kpb.reshape(b, s, ATT_W), vpt, lams, g_subln, lam0).reshape(b * s, ATT_W)
    k_prompt = kpt.reshape(1, b, N_HEADS, 2, HEAD_DIM, s).transpose(0, 1, 5, 2, 3, 4)
    mix_p = _mix_weights(sgu_w[0], sgu_b[0], CHUNK)
    yp, ap = _post(xp, op, up, vnp, gap, gbp, *mix_p, w_o_b, g_mix_post, jnp.zeros((SUBLANES, LANES), F32), *ffn_w,
                   tiles_per_row=s // TM, sample=False)
    conv_p = ap.reshape(b, s // TM, SUBLANES, D_FF)[:, -1, SUBLANES - (CONV_W - 1):]

    xs = x_sample.reshape(db * t, D_MODEL)
    pos_s = jnp.tile(past_len + jnp.arange(t), TM // t)
    qs, ksf, ksb, vsf, vsb, us, vns, gas, gbs = _in_proj(
        xs, *proj_w, _rope_tables(pos_s), q_dtype=F32, vn_dtype=F32)
    pad_new = lambda a: jnp.pad(a.reshape(db, t, ATT_W), ((0, 0), (0, NEW_PAD - t), (0, 0)))
    ckt = cache_k[0].transpose(0, 2, 3, 4, 1).reshape(n_pool, ATT_W, PAGE_SIZE)
    cv2 = cache_v[0].reshape(n_pool, PAGE_SIZE * N_HEADS, V_DIM)
    os_ = _sample_attn(qs, pad_new(ksb), pad_new(vsb), ckt, cv2, page_table, lams, g_subln, lam0)
    mix_s = _mix_weights(sgu_w[0], sgu_b[0], t)
    prev = jnp.pad(state_conv[0], ((0, 0), (0, t - (CONV_W - 1)), (0, 0))).reshape(db * t, D_FF)
    ys, as_ = _post(xs, os_, us, vns, gas, gbs, *mix_s, w_o_b, g_mix_post, prev, *ffn_w,
                    tiles_per_row=1, sample=True)
    conv_s = as_.reshape(db, t, D_FF)[:, t - (CONV_W - 1):]

    return (yp.reshape(b, s, D_MODEL), ys.reshape(db, t, D_MODEL),
            k_prompt, vpf.reshape(1, b, s, N_HEADS, V_DIM),
            conv_p[None], ksf.reshape(1, db, t, N_HEADS, 2, HEAD_DIM), vsf.reshape(1, db, t, N_HEADS, V_DIM),
            vns.reshape(1, db, t, D_SGU), conv_s[None])
```

```python
import functools
import math

import jax
import jax.numpy as jnp
from jax import lax
from jax.experimental import pallas as pl
from jax.experimental.pallas import tpu as pltpu

F32 = jnp.float32
BF16 = jnp.bfloat16

D_MODEL = 1024
N_HEADS = 8
HEAD_DIM = 64
V_DIM = 2 * HEAD_DIM
ATT_W = N_HEADS * 2 * HEAD_DIM
ROT_DIM = HEAD_DIM // 4
ROPE_THETA = 500000.0
SGU_GROUPS = 8
CHUNK = 128
D_SGU = 1024
D_FF = 2816
CONV_W = 3
EPS = 1e-6
PAGE_SIZE = 128
D_IN = 3 * ATT_W + 2 * D_SGU + 2 * D_MODEL
NEG = -1e30

LANES = 128
SUBLANES = 8
VMEM_LIMIT = 56 * 1024 * 1024

TM = 256
TQ = 256
HEADS_PER_STEP = 8
PAGES_PER_STEP = 16
NEW_PAD = 16
ONES_ROWS = 16
FF_CHUNKS = ((0, 1536), (1536, D_FF))
LOG2E = 1.4426950408889634


def _rms(x, g):
    return x * lax.rsqrt(jnp.mean(x * x, axis=-1, keepdims=True) + EPS) * g


def _nt_dot(a, b):
    return lax.dot_general(a, b, (((1,), (1,)), ((), ())), preferred_element_type=F32)


def _full(shape):
    return pl.BlockSpec(shape, lambda i: (0,) * len(shape), pipeline_mode=pl.Buffered(1))


def _in_proj_kernel(x_ref, w_ref, gpre_ref, lng_ref, lnb_ref, cos_ref, sa_ref, sb_ref,
                    q_ref, kf_ref, kb_ref, vf_ref, vb_ref, u_ref, vn_ref, ga_ref, gb_ref, *, transposed):
    xn = _rms(x_ref[...], gpre_ref[...]).astype(BF16)
    q_scale = HEAD_DIM ** -0.5 * (LOG2E if transposed else 1.0)
    heads = [slice(h * V_DIM, (h + 1) * V_DIM) for h in range(N_HEADS)]

    def proj(seg):
        return jnp.dot(xn, w_ref[:, seg * ATT_W:(seg + 1) * ATT_W], preferred_element_type=F32)

    cos, sa, sb = cos_ref[...], sa_ref[...], sb_ref[...]

    def rope(p, h):
        s = p[:, heads[h]]
        return s * cos + pltpu.roll(s, LANES - ROT_DIM // 2, 1) * sa + pltpu.roll(s, ROT_DIM // 2, 1) * sb

    p = proj(0)
    for h in range(N_HEADS):
        r = rope(p, h) * q_scale
        if transposed:
            q_ref[heads[h], :] = r.T.astype(q_ref.dtype)
        else:
            q_ref[:, heads[h]] = r.astype(q_ref.dtype)
    p = proj(1)
    for h in range(N_HEADS):
        r = rope(p, h)
        if transposed:
            kf_ref[heads[h], :] = r.T
        else:
            kf_ref[:, heads[h]] = r
        kb_ref[:, heads[h]] = r.astype(BF16)
    p = proj(2)
    vf_ref[...] = p
    if transposed:
        for h in range(N_HEADS):
            vb_ref[heads[h], :] = p[:, heads[h]].T.astype(BF16)
    else:
        vb_ref[...] = p.astype(BF16)
    u_ref[...] = jax.nn.gelu(proj(3)).astype(BF16)
    g = jax.nn.gelu(proj(4))
    mu = jnp.mean(g, axis=-1, keepdims=True)
    gc = g - mu
    vn = gc * lax.rsqrt(jnp.mean(gc * gc, axis=-1, keepdims=True) + EPS) * lng_ref[...] + lnb_ref[...]
    vn_ref[...] = vn.astype(vn_ref.dtype)
    ga_ref[...] = jax.nn.sigmoid(proj(5)).astype(BF16)
    gb_ref[...] = jax.nn.sigmoid(proj(6)).astype(BF16)


def _in_proj(x, w_in, g_pre, ln_g, ln_b, tabs, *, q_dtype, vn_dtype, seq_len=None):
    n = x.shape[0]
    n_tab = tabs[0].shape[0] // TM
    row = lambda i: (i, 0)
    tab = lambda i: (i % n_tab, 0)
    act = pl.BlockSpec((TM, D_MODEL), row)
    outs = [(q_dtype, ATT_W), (F32, ATT_W), (BF16, ATT_W), (F32, ATT_W), (BF16, ATT_W),
            (BF16, D_SGU), (vn_dtype, D_SGU), (BF16, D_MODEL), (BF16, D_MODEL)]
    out_specs = [pl.BlockSpec((TM, w), row) for _, w in outs]
    out_shape = [jax.ShapeDtypeStruct((n, w), d) for d, w in outs]
    if seq_len is not None:
        tpr = seq_len // TM
        for idx in (0, 1, 4):
            out_specs[idx] = pl.BlockSpec((None, ATT_W, TM), lambda i: (i // tpr, 0, i % tpr))
            out_shape[idx] = jax.ShapeDtypeStruct((n // seq_len, ATT_W, seq_len), outs[idx][0])
    return pl.pallas_call(
        functools.partial(_in_proj_kernel, transposed=seq_len is not None),
        grid=(n // TM,),
        in_specs=[act, _full((D_MODEL, D_IN)), _full((1, D_MODEL)), _full((1, D_SGU)), _full((1, D_SGU)),
                  pl.BlockSpec((TM, LANES), tab), pl.BlockSpec((TM, LANES), tab), pl.BlockSpec((TM, LANES), tab)],
        out_specs=out_specs,
        out_shape=out_shape,
        compiler_params=pltpu.CompilerParams(dimension_semantics=("parallel",), vmem_limit_bytes=VMEM_LIMIT),
        name="in_proj",
    )(x, w_in, g_pre, ln_g, ln_b, *tabs)


def _rope_tables(pos):
    half = ROT_DIM // 2
    inv = ROPE_THETA ** (-jnp.arange(0, ROT_DIM, 2, dtype=F32) / ROT_DIM)
    ang = pos.astype(F32)[:, None] * inv[None, :]
    cos, sin = jnp.cos(ang), jnp.sin(ang)
    d = jnp.arange(LANES) % HEAD_DIM
    lo, hi = d < half, (d >= half) & (d < ROT_DIM)
    idx = d % half
    c = jnp.where((lo | hi)[None, :], cos[:, idx], 1.0)
    sa = jnp.where(lo[None, :], -sin[:, idx], 0.0)
    sb = jnp.where(hi[None, :], sin[:, idx], 0.0)
    return c, sa, sb


def _diff_lambda(lq1, lk1, lq2, lk2, lam0):
    s1 = jnp.sum(lq1[...] * lk1[...], axis=-1, keepdims=True)
    s2 = jnp.sum(lq2[...] * lk2[...], axis=-1, keepdims=True)
    return jnp.exp(s1) - jnp.exp(s2) + lam0


def _subln(o, g, lam0):
    return _rms(o, g) * (1.0 - lam0)


def _prompt_attn_kernel(qt_ref, k_ref, vt_ref, lq1, lk1, lq2, lk2, gsub_ref, o_ref, m_ref, acc_ref, s_ref, *, lam0):
    hps = HEADS_PER_STEP
    qi = pl.program_id(2)
    feat = lax.broadcasted_iota(jnp.int32, (V_DIM, TQ), 0)
    hsl = [slice(hh * V_DIM, (hh + 1) * V_DIM) for hh in range(hps)]
    qst = []
    for hh in range(hps):
        qt = qt_ref[hsl[hh], :]
        zero = jnp.zeros_like(qt)
        qst.append(jnp.concatenate([jnp.where(feat < HEAD_DIM, qt, zero), jnp.where(feat >= HEAD_DIM, qt, zero)],
                                   axis=1))
    m_ref[...] = jnp.full(m_ref.shape, NEG, F32)
    acc_ref[...] = jnp.zeros(acc_ref.shape, F32)
    ones = jnp.ones((ONES_ROWS, TQ), BF16)

    def scores(kv, slot):
        start = pl.multiple_of(kv * TQ, TQ)
        for hh in range(hps):
            s_ref[slot, hh] = jnp.dot(k_ref[pl.ds(start, TQ), hsl[hh]], qst[hh], preferred_element_type=F32)

    def softmax_pv(kv, slot, masked):
        start = pl.multiple_of(kv * TQ, TQ)
        ps, corrs = [], []
        for hh in range(hps):
            s = s_ref[slot, hh]
            if masked:
                key = lax.broadcasted_iota(jnp.int32, s.shape, 0)
                qry = lax.broadcasted_iota(jnp.int32, s.shape, 1) % TQ
                s = jnp.where(key <= qry, s, NEG)
            m_old = m_ref[hh]
            m_new = jnp.maximum(m_old, jnp.max(s, axis=0, keepdims=True))
            corrs.append(jnp.exp2(m_old - m_new))
            ps.append(jnp.exp2(s - m_new).astype(BF16))
            m_ref[hh] = m_new
        for hh in range(hps):
            vt1 = jnp.concatenate([vt_ref[hsl[hh], pl.ds(start, TQ)], ones], axis=0)
            acc_ref[hh] = acc_ref[hh] * corrs[hh] + jnp.dot(vt1, ps[hh], preferred_element_type=F32)

    scores(0, 0)

    def pair(j, carry):
        kv = 2 * j
        scores(kv + 1, 1)
        softmax_pv(kv, 0, False)
        scores(kv + 2, 0)
        softmax_pv(kv + 1, 1, False)
        return carry

    lax.fori_loop(0, qi // 2, pair, 0)

    @pl.when(qi % 2 == 1)
    def _odd():
        scores(qi, 1)
        softmax_pv(qi - 1, 0, False)
        softmax_pv(qi, 1, True)

    @pl.when(qi % 2 == 0)
    def _even():
        softmax_pv(qi, 0, True)

    lam = _diff_lambda(lq1, lk1, lq2, lk2, lam0)
    for hh in range(hps):
        acc = acc_ref[hh]
        nrm = acc[:V_DIM] / acc[V_DIM:V_DIM + 1]
        ot = nrm[:, :TQ] - lam * nrm[:, TQ:]
        o_ref[:, hsl[hh]] = _subln(ot.T, gsub_ref[...], lam0).astype(o_ref.dtype)


def _prompt_attn(qt, k, vt, lams, g_sub, lam0):
    b, s, _ = k.shape
    hps = HEADS_PER_STEP
    w = hps * V_DIM
    qblk = pl.BlockSpec((None, w, TQ), lambda bi, h, qi: (bi, h, qi))
    vblk = pl.BlockSpec((None, w, s), lambda bi, h, qi: (bi, h, 0))
    kblk = pl.BlockSpec((None, s, w), lambda bi, h, qi: (bi, 0, h))
    oblk = pl.BlockSpec((None, TQ, w), lambda bi, h, qi: (bi, qi, h))
    small = lambda shape: pl.BlockSpec(shape, lambda bi, h, qi: (0, 0))
    return pl.pallas_call(
        functools.partial(_prompt_attn_kernel, lam0=lam0),
        grid=(b, N_HEADS // hps, s // TQ),
        in_specs=[qblk, kblk, vblk] + [small((1, HEAD_DIM))] * 4 + [small((1, V_DIM))],
        out_specs=oblk,
        out_shape=jax.ShapeDtypeStruct((b, s, ATT_W), BF16),
        scratch_shapes=[pltpu.VMEM((hps, 1, 2 * TQ), F32), pltpu.VMEM((hps, V_DIM + ONES_ROWS, 2 * TQ), F32),
                        pltpu.VMEM((2, hps, TQ, 2 * TQ), F32)],
        compiler_params=pltpu.CompilerParams(dimension_semantics=("parallel", "parallel", "arbitrary"),
                                             vmem_limit_bytes=VMEM_LIMIT),
        name="prompt_attn",
    )(qt, k, vt, *lams, g_sub)


def _sample_attn_kernel(pt_ref, q_ref, kn_ref, vn_ref, lq1, lk1, lq2, lk2, gsub_ref, *rest, lam0):
    del pt_ref
    pp = PAGES_PER_STEP
    k_refs, v_refs = rest[:pp], rest[pp:2 * pp]
    o_ref, qh_ref, m_ref, l_ref, acc_ref = rest[2 * pp:]
    g = pl.program_id(1)
    hr = 2 * SUBLANES
    heads = [slice(h * hr, (h + 1) * hr) for h in range(N_HEADS)]
    lanes = [slice(h * V_DIM, (h + 1) * V_DIM) for h in range(N_HEADS)]

    @pl.when(g == 0)
    def _init():
        q = q_ref[...]
        lane = lax.broadcasted_iota(jnp.int32, (SUBLANES, V_DIM), 1)
        for h in range(N_HEADS):
            qh = q[:, lanes[h]]
            qh_ref[heads[h], :] = jnp.concatenate(
                [jnp.where(lane < HEAD_DIM, qh, 0.0), jnp.where(lane >= HEAD_DIM, qh, 0.0)], axis=0).astype(BF16)
        m_ref[...] = jnp.full(m_ref.shape, NEG, F32)
        l_ref[...] = jnp.zeros(l_ref.shape, F32)
        acc_ref[...] = jnp.zeros(acc_ref.shape, F32)

    def update(s, pv_all):
        m_old = m_ref[...]
        m_new = jnp.maximum(m_old, jnp.max(s, axis=1, keepdims=True))
        corr = jnp.exp(m_old - m_new)
        p = jnp.exp(s - m_new)
        l_ref[...] = l_ref[...] * corr + jnp.sum(p, axis=1, keepdims=True)
        m_ref[...] = m_new
        acc_ref[...] = acc_ref[...] * corr + pv_all(p.astype(BF16))

    def page_scores(h):
        kt = jnp.concatenate([r[lanes[h], :].astype(BF16) for r in k_refs], axis=1)
        return jnp.dot(qh_ref[heads[h], :], kt, preferred_element_type=F32)

    half = N_HEADS // 2
    tok = lax.broadcasted_iota(jnp.int32, (PAGE_SIZE, 2 * PAGE_SIZE), 0)
    col = lax.broadcasted_iota(jnp.int32, (PAGE_SIZE, 2 * PAGE_SIZE), 1)
    spread = [jnp.where(col == 2 * tok + j, 1.0, 0.0).astype(BF16) for j in range(2)]

    def page_pv_all(p):
        pex = []
        for j in range(2):
            rows = slice(j * half * hr, (j + 1) * half * hr)
            stack = jnp.concatenate([p[rows, pg * PAGE_SIZE:(pg + 1) * PAGE_SIZE] for pg in range(pp)], axis=0)
            pex.append(jnp.dot(stack, spread[j], preferred_element_type=F32).astype(BF16))
        outs = [None] * N_HEADS
        for h in range(half):
            lhs = jnp.concatenate(
                [jnp.concatenate([pex[j][pg * half * hr + h * hr: pg * half * hr + (h + 1) * hr, :] for j in range(2)],
                                 axis=0) for pg in range(pp)], axis=1)
            vp = jnp.concatenate([r[pl.ds(h, 2 * PAGE_SIZE, stride=half), :].astype(BF16) for r in v_refs], axis=0)
            o = jnp.dot(lhs, vp, preferred_element_type=F32)
            outs[h], outs[h + half] = o[:hr], o[hr:]
        return jnp.concatenate(outs, axis=0)

    update(jnp.concatenate([page_scores(h) for h in range(N_HEADS)], axis=0), page_pv_all)

    @pl.when(g == pl.num_programs(1) - 1)
    def _finish():
        s = jnp.concatenate([_nt_dot(qh_ref[heads[h], :], kn_ref[:, lanes[h]]) for h in range(N_HEADS)], axis=0)
        t = lax.broadcasted_iota(jnp.int32, s.shape, 0) % SUBLANES
        j = lax.broadcasted_iota(jnp.int32, s.shape, 1)
        s = jnp.where(j <= t, s, NEG)
        update(s, lambda p: jnp.concatenate(
            [jnp.dot(p[heads[h], :], vn_ref[:, lanes[h]], preferred_element_type=F32) for h in range(N_HEADS)], axis=0))
        lam = _diff_lambda(lq1, lk1, lq2, lk2, lam0)
        nrm = acc_ref[...] / l_ref[...]
        for h in range(N_HEADS):
            blk = nrm[heads[h], :]
            o = blk[:SUBLANES] - lam * blk[SUBLANES:]
            o_ref[:, lanes[h]] = _subln(o, gsub_ref[...], lam0)


def _sample_attn(q, k_new, v_new, cache_kt, cache_v2, page_table, lams, g_sub, lam0):
    db, n_pages = page_table.shape
    pp = PAGES_PER_STEP
    t = q.shape[0] // db
    rows = N_HEADS * 2 * SUBLANES
    kpage = lambda r: pl.BlockSpec((None, ATT_W, PAGE_SIZE), lambda b, g, pt: (pt[b, g * pp + r], 0, 0))
    vpage = lambda r: pl.BlockSpec((None, PAGE_SIZE * N_HEADS, V_DIM), lambda b, g, pt: (pt[b, g * pp + r], 0, 0))
    small = lambda shape: pl.BlockSpec(shape, lambda b, g, pt: (0, 0))
    new = pl.BlockSpec((None, NEW_PAD, ATT_W), lambda b, g, pt: (b, 0, 0))
    qblk = pl.BlockSpec((t, ATT_W), lambda b, g, pt: (b, 0))
    grid_spec = pltpu.PrefetchScalarGridSpec(
        num_scalar_prefetch=1,
        grid=(db, n_pages // pp),
        in_specs=[qblk, new, new] + [small((1, HEAD_DIM))] * 4 + [small((1, V_DIM))]
                 + [kpage(r) for r in range(pp)] + [vpage(r) for r in range(pp)],
        out_specs=qblk,
        scratch_shapes=[pltpu.VMEM((rows, V_DIM), BF16), pltpu.VMEM((rows, 1), F32), pltpu.VMEM((rows, 1), F32),
                        pltpu.VMEM((rows, V_DIM), F32)],
    )
    return pl.pallas_call(
        functools.partial(_sample_attn_kernel, lam0=lam0),
        grid_spec=grid_spec,
        out_shape=jax.ShapeDtypeStruct(q.shape, F32),
        compiler_params=pltpu.CompilerParams(dimension_semantics=("parallel", "arbitrary"),
                                             vmem_limit_bytes=VMEM_LIMIT),
        name="sample_attn",
    )(page_table, q, k_new, v_new, *lams, g_sub, *([cache_kt] * pp), *([cache_v2] * pp))


def _mix_weights(sgu_w, sgu_b, chunk_len):
    wm = (sgu_w * jnp.tril(jnp.ones((CHUNK, CHUNK), sgu_w.dtype)))[:, :chunk_len, :chunk_len]
    n = TM // chunk_len
    tiled = jnp.concatenate([jnp.concatenate([wm] * n, axis=2)] * n, axis=1)
    blk = jnp.arange(TM) // chunk_len
    mixw = jnp.where((blk[:, None] == blk[None, :])[None], tiled, 0.0).astype(BF16)
    bias = jnp.repeat(sgu_b[:, :chunk_len].T, D_SGU // SGU_GROUPS, axis=1)
    return mixw, jnp.tile(bias, (TM // chunk_len, 1))


def _post_kernel(x_ref, o_ref, u_ref, vn_ref, ga_ref, gb_ref, mixw_ref, mixb_ref, wo_ref, gmix_ref,
                 prev_ref, gpre_ref, wup_ref, cw_ref, cb_ref, wdn_ref, gpost_ref,
                 y_ref, a_ref, carry_ref, *, tiles_per_row, sample):
    vn = vn_ref[...].astype(BF16)
    parts = []
    for g in range(SGU_GROUPS):
        z = jnp.dot(mixw_ref[g], vn[:, g * LANES:(g + 1) * LANES], preferred_element_type=F32)
        parts.append(z + mixb_ref[:, g * LANES:(g + 1) * LANES])
    s_out = u_ref[...].astype(F32) * jnp.concatenate(parts, axis=1)
    merged = ga_ref[...].astype(F32) * o_ref[...].astype(F32) + gb_ref[...].astype(F32) * s_out
    h = x_ref[...] + _rms(jnp.dot(merged.astype(BF16), wo_ref[...], preferred_element_type=F32), gmix_ref[...])

    hn = _rms(h, gpre_ref[...]).astype(BF16)
    if not sample:
        @pl.when(pl.program_id(0) % tiles_per_row == 0)
        def _():
            carry_ref[...] = jnp.zeros(carry_ref.shape, F32)

    out = None
    for lo, hi in FF_CHUNKS:
        a = jnp.dot(hn, wup_ref[:, lo:hi], preferred_element_type=F32)
        b = jnp.dot(hn, wup_ref[:, D_FF + lo:D_FF + hi], preferred_element_type=F32)
        am1 = pltpu.roll(a, 1, 0)
        am2 = pltpu.roll(a, 2, 0)
        if sample:
            prev = prev_ref[:, lo:hi]
            t = lax.broadcasted_iota(jnp.int32, a.shape, 0) % SUBLANES
            am1 = jnp.where(t == 0, pltpu.roll(prev, TM - 1, 0), am1)
            am2 = jnp.where(t < 2, prev, am2)
            a_ref[:, lo:hi] = a
        else:
            last = carry_ref[:, lo:hi]
            t = lax.broadcasted_iota(jnp.int32, last.shape, 0)
            head1 = jnp.where(t == 0, pltpu.roll(last, 1, 0), am1[:SUBLANES])
            head2 = jnp.where(t < 2, pltpu.roll(last, 2, 0), am2[:SUBLANES])
            am1 = jnp.concatenate([head1, am1[SUBLANES:]], axis=0)
            am2 = jnp.concatenate([head2, am2[SUBLANES:]], axis=0)
            carry_ref[:, lo:hi] = a[TM - SUBLANES:]
            a_ref[:, lo:hi] = a[TM - SUBLANES:]
        c = cb_ref[:, lo:hi] + cw_ref[0:1, lo:hi] * am2 + cw_ref[1:2, lo:hi] * am1 + cw_ref[2:3, lo:hi] * a
        gated = (jax.nn.gelu(c) * b).astype(BF16)
        part = jnp.dot(gated, wdn_ref[lo:hi, :], preferred_element_type=F32)
        out = part if out is None else out + part
    y_ref[...] = h + _rms(out, gpost_ref[...])


def _post(x, o, u, vn, ga, gb, mixw, mixb, w_o, g_mix_post, prev, g_pre, w_up, conv_w, conv_b, w_down, g_post,
          *, tiles_per_row, sample):
    n = x.shape[0]
    row = lambda i: (i, 0)
    act = pl.BlockSpec((TM, D_MODEL), row)
    if sample:
        prev_spec = pl.BlockSpec((TM, D_FF), row)
        a_spec = pl.BlockSpec((TM, D_FF), row)
        a_shape = jax.ShapeDtypeStruct((n, D_FF), F32)
    else:
        prev_spec = _full(prev.shape)
        a_spec = pl.BlockSpec((None, SUBLANES, D_FF), lambda i: (i, 0, 0))
        a_shape = jax.ShapeDtypeStruct((n // TM, SUBLANES, D_FF), F32)
    return pl.pallas_call(
        functools.partial(_post_kernel, tiles_per_row=tiles_per_row, sample=sample),
        grid=(n // TM,),
        in_specs=[act] * 6 + [_full((SGU_GROUPS, TM, TM)), _full((TM, D_SGU)), _full((D_MODEL, D_MODEL)),
                              _full((1, D_MODEL)), prev_spec, _full((1, D_MODEL)), _full((D_MODEL, 2 * D_FF)),
                              _full((CONV_W, D_FF)), _full((1, D_FF)), _full((D_FF, D_MODEL)), _full((1, D_MODEL))],
        out_specs=[act, a_spec],
        out_shape=[jax.ShapeDtypeStruct((n, D_MODEL), F32), a_shape],
        scratch_shapes=[pltpu.VMEM((SUBLANES, D_FF), F32)],
        compiler_params=pltpu.CompilerParams(dimension_semantics=("arbitrary",), vmem_limit_bytes=VMEM_LIMIT),
        name="post",
    )(x, o, u, vn, ga, gb, mixw, mixb, w_o, g_mix_post, prev, g_pre, w_up, conv_w, conv_b, w_down, g_post)


def kernel(x_prompt, x_sample, cache_k, cache_v, state_conv, page_table, w_in, w_o, g_mix_pre, g_mix_post,
           lambda_q1, lambda_k1, lambda_q2, lambda_k2, g_subln, sgu_ln_g, sgu_ln_b, sgu_w, sgu_b, g_ffn_pre,
           g_ffn_post, w_up, conv_w, conv_b, w_down):
    depth = w_in.shape[0]
    assert depth == 1, "one layer per step"
    b, s, _ = x_prompt.shape
    db, t, _ = x_sample.shape
    n_pool, n_pages = cache_k.shape[1], page_table.shape[1]
    past_len = n_pages * PAGE_SIZE
    assert s % TM == 0 and s % TQ == 0 and (db * t) % TM == 0 and t == SUBLANES and TM % CHUNK == 0
    assert n_pages % PAGES_PER_STEP == 0 and N_HEADS % HEADS_PER_STEP == 0

    lam0 = 0.8 - 0.6 * math.exp(-0.3 * 0)
    lams = (lambda_q1, lambda_k1, lambda_q2, lambda_k2)
    w_in_b, w_o_b = w_in[0].astype(BF16), w_o[0].astype(BF16)
    w_up_b, w_down_b = w_up[0].astype(BF16), w_down[0].astype(BF16)
    proj_w = (w_in_b, g_mix_pre, sgu_ln_g, sgu_ln_b)
    ffn_w = (g_ffn_pre, w_up_b, conv_w[0], conv_b, w_down_b, g_ffn_post)

    xp = x_prompt.reshape(b * s, D_MODEL)
    qpt, kpt, kpb, vpf, vpt, up, vnp, gap, gbp = _in_proj(
        xp, *proj_w, _rope_tables(jnp.arange(s)), q_dtype=BF16, vn_dtype=BF16, seq_len=s)
    op = _prompt_attn(qpt, kpb.reshape(b, s, ATT_W), vpt, lams, g_subln, lam0).reshape(b * s, ATT_W)
    k_prompt = kpt.reshape(1, b, N_HEADS, 2, HEAD_DIM, s).transpose(0, 1, 5, 2, 3, 4)
    mix_p = _mix_weights(sgu_w[0], sgu_b[0], CHUNK)
    yp, ap = _post(xp, op, up, vnp, gap, gbp, *mix_p, w_o_b, g_mix_post, jnp.zeros((SUBLANES, LANES), F32), *ffn_w,
                   tiles_per_row=s // TM, sample=False)
    conv_p = ap.reshape(b, s // TM, SUBLANES, D_FF)[:, -1, SUBLANES - (CONV_W - 1):]

    xs = x_sample.reshape(db * t, D_MODEL)
    pos_s = jnp.tile(past_len + jnp.arange(t), TM // t)
    qs, ksf, ksb, vsf, vsb, us, vns, gas, gbs = _in_proj(
        xs, *proj_w, _rope_tables(pos_s), q_dtype=F32, vn_dtype=F32)
    pad_new = lambda a: jnp.pad(a.reshape(db, t, ATT_W), ((0, 0), (0, NEW_PAD - t), (0, 0)))
    ckt = cache_k[0].transpose(0, 2, 3, 4, 1).reshape(n_pool, ATT_W, PAGE_SIZE)
    cv2 = cache_v[0].reshape(n_pool, PAGE_SIZE * N_HEADS, V_DIM)
    os_ = _sample_attn(qs, pad_new(ksb), pad_new(vsb), ckt, cv2, page_table, lams, g_subln, lam0)
    mix_s = _mix_weights(sgu_w[0], sgu_b[0], t)
    prev = jnp.pad(state_conv[0], ((0, 0), (0, t - (CONV_W - 1)), (0, 0))).reshape(db * t, D_FF)
    ys, as_ = _post(xs, os_, us, vns, gas, gbs, *mix_s, w_o_b, g_mix_post, prev, *ffn_w,
                    tiles_per_row=1, sample=True)
    conv_s = as_.reshape(db, t, D_FF)[:, t - (CONV_W - 1):]

    return (yp.reshape(b, s, D_MODEL), ys.reshape(db, t, D_MODEL),
            k_prompt, vpf.reshape(1, b, s, N_HEADS, V_DIM),
            conv_p[None], ksf.reshape(1, db, t, N_HEADS, 2, HEAD_DIM), vsf.reshape(1, db, t, N_HEADS, V_DIM),
            vns.reshape(1, db, t, D_SGU), conv_s[None])
```
